```python
import jax, jax.numpy as jnp
from jax import lax
import numpy as np

D_MODEL = 1024
BATCH = 2
SEQ = 8192
DEPTH = 1
DEC_BATCH = 32
DEC_SEQ = 8
PAST_LEN = 8192
PAGE_SIZE = 128

SB_HEADS = 8
SB_HEAD_DIM = 64
SB_WIDTH = SB_HEADS * SB_HEAD_DIM
SB_BLOCK = 128
SB_BIAS_INIT = -6.0
GLA_HEADS = 4
GLA_KEY_DIM = 64
GLA_VAL_DIM = 128
GLA_QK_WIDTH = GLA_HEADS * GLA_KEY_DIM
GLA_V_WIDTH = GLA_HEADS * GLA_VAL_DIM
GLA_GATE_RANK = 16
GLA_GATE_TEMP = 16.0
GLA_CHUNK = 16
MIX_WIDTH = SB_WIDTH + GLA_V_WIDTH
D_FF = 4 * D_MODEL
NORM_EPS = 1e-6
IN_SPLITS = (SB_WIDTH, SB_WIDTH, SB_WIDTH, GLA_QK_WIDTH, GLA_QK_WIDTH, GLA_V_WIDTH, GLA_V_WIDTH, GLA_GATE_RANK)
IN_WIDTH = 3 * SB_WIDTH + 2 * GLA_QK_WIDTH + 2 * GLA_V_WIDTH + GLA_GATE_RANK

kernel_name = "stickbreak_gla_hybrid_step"


def rms_norm(x, gain):
    xf = x.astype(jnp.float32)
    y = xf * lax.rsqrt(jnp.mean(xf * xf, axis=-1, keepdims=True) + NORM_EPS)
    return (y * gain.astype(jnp.float32)).astype(x.dtype)


def split_columns(p):
    parts, start = [], 0
    for w in IN_SPLITS:
        parts.append(p[..., start:start + w])
        start += w
    return parts


def sb_block(q, k, v, q_pos, k_pos, bias):
    z = (jnp.einsum("bqhd,bkhd->bhqk", q, k).astype(jnp.float32) * (SB_HEAD_DIM ** -0.5)
         + bias.astype(jnp.float32)[None, :, None, None])
    causal = (k_pos[None, :] < q_pos[:, None])[None, None]
    log_stay = jnp.where(causal, jax.nn.log_sigmoid(-z), 0.0)
    log_after = lax.cumsum(log_stay, axis=3, reverse=True) - log_stay
    w = jnp.where(causal, jnp.exp(jax.nn.log_sigmoid(z) + log_after), 0.0)
    return jnp.einsum("bhqk,bkhd->bqhd", w.astype(v.dtype), v)


def stick_breaking_attention(q, k, v, q_offset, bias):
    B, Tq, H, Dh = q.shape
    Tk = k.shape[1]
    blk = SB_BLOCK if Tq % SB_BLOCK == 0 else Tq
    n = Tq // blk
    k_pos = jnp.arange(Tk, dtype=jnp.int32)
    q_pos = (q_offset + jnp.arange(Tq, dtype=jnp.int32)).reshape(n, blk)
    qb = q.reshape(B, n, blk, H, Dh).transpose(1, 0, 2, 3, 4)
    ob = lax.map(lambda a: sb_block(a[0], k, v, a[1], k_pos, bias), (qb, q_pos))
    return ob.transpose(1, 0, 2, 3, 4).reshape(B, Tq, H, Dh)


def gla_chunked(q, k, v, log_a, s0):
    B, T, H, DK = q.shape
    DV = v.shape[-1]
    C = GLA_CHUNK if T % GLA_CHUNK == 0 else T
    n = T // C

    def blocks(t):
        return t.astype(jnp.float32).reshape(B, n, C, H, t.shape[-1]).transpose(0, 3, 1, 2, 4)

    qc = blocks(q) * (DK ** -0.5)
    kc, vc, ac = blocks(k), blocks(v), blocks(log_a)
    b = jnp.cumsum(ac, axis=3)
    b_end = b[:, :, :, -1:, :]
    q_dec = qc * jnp.exp(b)
    k_inv = kc * jnp.exp(-b)
    tri = jnp.tril(jnp.ones((C, C), dtype=bool))
    scores = jnp.where(tri, jnp.einsum("bhncd,bhnsd->bhncs", q_dec, k_inv), 0.0)
    o_intra = jnp.einsum("bhncs,bhnsv->bhncv", scores, vc)
    ds = jnp.einsum("bhncd,bhncv->nbhdv", kc * jnp.exp(b_end - b), vc)
    decay = jnp.exp(b_end[:, :, :, 0, :]).transpose(2, 0, 1, 3)

    def step(s, inp):
        dec, d = inp
        return dec[..., None] * s + d, s

    s_final, s_start = lax.scan(step, s0.astype(jnp.float32), (decay, ds))
    o_inter = jnp.einsum("bhncd,nbhdv->bhncv", q_dec, s_start)
    o = (o_intra + o_inter).transpose(0, 2, 3, 1, 4).reshape(B, T, H, DV)
    return o.astype(v.dtype), s_final


def hybrid_layer(x, c, k_past, v_past, gla_s0, q_offset, w_ada, b_ada, g_mix, w_in, b_sb,
                 w_gla_gate, b_gla_gate, g_gla_out, w_out, g_ffn, w_up, w_down):
    B, T, _ = x.shape
    mod = jax.nn.silu(c) @ w_ada + b_ada
    sh1, sc1, ga1, sh2, sc2, ga2 = [m[:, None, :] for m in jnp.split(mod, 6, axis=-1)]
    h = rms_norm(x, g_mix) * (1.0 + sc1) + sh1
    q_sb, k_sb, v_sb, q_g, k_g, v_g, r_g, a_low = split_columns(h @ w_in)
    q_sb = q_sb.reshape(B, T, SB_HEADS, SB_HEAD_DIM)
    k_sb = k_sb.reshape(B, T, SB_HEADS, SB_HEAD_DIM)
    v_sb = v_sb.reshape(B, T, SB_HEADS, SB_HEAD_DIM)
    k_all = k_sb if k_past is None else jnp.concatenate([k_past, k_sb], axis=1)
    v_all = v_sb if v_past is None else jnp.concatenate([v_past, v_sb], axis=1)
    o_sb = stick_breaking_attention(q_sb, k_all, v_all, q_offset, b_sb).reshape(B, T, SB_WIDTH)
    log_a = jax.nn.log_sigmoid((a_low @ w_gla_gate + b_gla_gate).astype(jnp.float32)) / GLA_GATE_TEMP
    o_g, s_new = gla_chunked(q_g.reshape(B, T, GLA_HEADS, GLA_KEY_DIM),
                             k_g.reshape(B, T, GLA_HEADS, GLA_KEY_DIM),
                             v_g.reshape(B, T, GLA_HEADS, GLA_VAL_DIM),
                             log_a.reshape(B, T, GLA_HEADS, GLA_KEY_DIM), gla_s0)
    o_g = rms_norm(o_g, g_gla_out).reshape(B, T, GLA_V_WIDTH) * jax.nn.silu(r_g)
    x = x + ga1 * (jnp.concatenate([o_sb, o_g], axis=-1) @ w_out)
    h2 = rms_norm(x, g_ffn) * (1.0 + sc2) + sh2
    x = x + ga2 * (jnp.square(jax.nn.relu(h2 @ w_up)) @ w_down)
    return x, k_sb, v_sb, s_new


def setup_inputs(seed: int = 0) -> dict:
    key = jax.random.key(seed)
    ks = jax.random.split(key, 24)
    n_pages = PAST_LEN // PAGE_SIZE
    n_used = DEC_BATCH * n_pages
    n_phys = n_used + max(1, n_used // 4)
    f32 = jnp.float32
    nrm = lambda k, s, sc: jax.random.normal(k, s, f32) * sc
    perm = jax.random.permutation(ks[7], n_phys)[:n_used].astype(jnp.int32)
    return {
        "x_prompt": nrm(ks[0], (BATCH, SEQ, D_MODEL), 1.0),
        "x_sample": nrm(ks[1], (DEC_BATCH, DEC_SEQ, D_MODEL), 1.0),
        "c_prompt": nrm(ks[2], (BATCH, D_MODEL), 1.0),
        "c_sample": nrm(ks[3], (DEC_BATCH, D_MODEL), 1.0),
        "cache_k": nrm(ks[4], (DEPTH, n_phys, PAGE_SIZE, SB_HEADS, SB_HEAD_DIM), 1.0),
        "cache_v": nrm(ks[5], (DEPTH, n_phys, PAGE_SIZE, SB_HEADS, SB_HEAD_DIM), 1.0),
        "state_gla": nrm(ks[6], (DEPTH, DEC_BATCH, GLA_HEADS, GLA_KEY_DIM, GLA_VAL_DIM), 0.3),
        "page_table": perm.reshape(DEC_BATCH, n_pages),
        "w_ada": nrm(ks[8], (DEPTH, D_MODEL, 6 * D_MODEL), 0.5 * D_MODEL ** -0.5),
        "b_ada": nrm(ks[9], (DEPTH, 6 * D_MODEL), 0.01),
        "g_mix": 1.0 + nrm(ks[10], (DEPTH, D_MODEL), 0.02),
        "w_in": nrm(ks[11], (DEPTH, D_MODEL, IN_WIDTH), D_MODEL ** -0.5),
        "b_sb": SB_BIAS_INIT + nrm(ks[20], (DEPTH, SB_HEADS), 0.5),
        "w_gla_gate": nrm(ks[12], (DEPTH, GLA_GATE_RANK, GLA_QK_WIDTH), GLA_GATE_RANK ** -0.5),
        "b_gla_gate": nrm(ks[13], (DEPTH, GLA_QK_WIDTH), 0.01),
        "g_gla_out": 1.0 + nrm(ks[14], (DEPTH, GLA_VAL_DIM), 0.02),
        "w_out": nrm(ks[15], (DEPTH, MIX_WIDTH, D_MODEL), MIX_WIDTH ** -0.5),
        "g_ffn": 1.0 + nrm(ks[16], (DEPTH, D_MODEL), 0.02),
        "w_up": nrm(ks[17], (DEPTH, D_MODEL, D_FF), D_MODEL ** -0.5),
        "w_down": nrm(ks[18], (DEPTH, D_FF, D_MODEL), D_FF ** -0.5),
        "g_final": 1.0 + nrm(ks[19], (D_MODEL,), 0.02),
    }


def reference(x_prompt, x_sample, c_prompt, c_sample, cache_k, cache_v, state_gla, page_table,
              w_ada, b_ada, g_mix, w_in, b_sb, w_gla_gate, b_gla_gate, g_gla_out, w_out,
              g_ffn, w_up, w_down, g_final):
    n_dec = x_sample.shape[0]
    past_len = page_table.shape[1] * cache_k.shape[2]
    xp, xs = x_prompt, x_sample
    kp_l, vp_l, sp_l, ks_l, vs_l, ss_l = [], [], [], [], [], []
    for l in range(DEPTH):
        wl = (w_ada[l], b_ada[l], g_mix[l], w_in[l], b_sb[l], w_gla_gate[l], b_gla_gate[l],
              g_gla_out[l], w_out[l], g_ffn[l], w_up[l], w_down[l])
        s0 = jnp.zeros((xp.shape[0], GLA_HEADS, GLA_KEY_DIM, GLA_VAL_DIM), jnp.float32)
        xp, kp, vp, sp = hybrid_layer(xp, c_prompt, None, None, s0, 0, *wl)
        k_past = cache_k[l][page_table].reshape(n_dec, past_len, SB_HEADS, SB_HEAD_DIM)
        v_past = cache_v[l][page_table].reshape(n_dec, past_len, SB_HEADS, SB_HEAD_DIM)
        xs, kn, vn, sn = hybrid_layer(xs, c_sample, k_past, v_past, state_gla[l], past_len, *wl)
        kp_l.append(kp); vp_l.append(vp); sp_l.append(sp)
        ks_l.append(kn); vs_l.append(vn); ss_l.append(sn)
    y_prompt = rms_norm(xp, g_final)
    y_sample = rms_norm(xs, g_final)
    return (y_prompt, y_sample, jnp.stack(kp_l), jnp.stack(vp_l), jnp.stack(sp_l),
            jnp.stack(ks_l), jnp.stack(vs_l), jnp.stack(ss_l))
```

```python
import functools

import numpy as np
import jax
import jax.numpy as jnp
from jax import lax
from jax.experimental import pallas as pl
from jax.experimental.pallas import tpu as pltpu

F32 = jnp.float32
BF16 = jnp.bfloat16

D_MODEL = 1024
SB_HEADS = 8
SB_HEAD_DIM = 64
SB_WIDTH = SB_HEADS * SB_HEAD_DIM
GLA_HEADS = 4
GLA_KEY_DIM = 64
GLA_VAL_DIM = 128
GLA_QK_WIDTH = GLA_HEADS * GLA_KEY_DIM
GLA_V_WIDTH = GLA_HEADS * GLA_VAL_DIM
GLA_GATE_RANK = 16
GLA_GATE_TEMP = 16.0
GLA_CHUNK = 16
D_FF = 4 * D_MODEL
NORM_EPS = 1e-6

LANES = 128
SUBLANES = 8
VMEM_LIMIT = 56 * 1024 * 1024

SB_BQ = 256
GLA_TB = 128
DEC_PAGES_PER_STEP = 16
ROW_TILE = 512
ADA_TN = 1024
FF_CHUNK = 1024

_NT = (((1,), (1,)), ((), ()))
_TN = (((0,), (0,)), ((), ()))


def _softplus(z):
    return jnp.maximum(z, 0.0) + jnp.log(1.0 + jnp.exp(-jnp.abs(z)))


def _silu(x):
    return x * jax.nn.sigmoid(x)


def _rms(x, gain):
    ms = jnp.mean(x * x, axis=-1, keepdims=True)
    return x * lax.rsqrt(ms + NORM_EPS) * gain


def _ada_body(c_ref, w_ref, b_ref, o_ref):
    s = _silu(c_ref[...])
    o_ref[...] = jnp.dot(s.astype(BF16), w_ref[...].astype(BF16),
                         preferred_element_type=F32) + b_ref[...]


def _ada(c, w_ada, b_ada):
    rows, d = c.shape
    n = w_ada.shape[1]
    return pl.pallas_call(
        _ada_body,
        grid=(n // ADA_TN,),
        in_specs=[pl.BlockSpec((rows, d), lambda j: (0, 0)),
                  pl.BlockSpec((d, ADA_TN), lambda j: (0, j)),
                  pl.BlockSpec((1, ADA_TN), lambda j: (0, j))],
        out_specs=pl.BlockSpec((rows, ADA_TN), lambda j: (0, j)),
        out_shape=jax.ShapeDtypeStruct((rows, n), F32),
        name="ada_mod",
    )(c, w_ada, b_ada.reshape(1, n))


def _inproj_body(x_ref, sh_ref, sc_ref, g_ref, wsb_ref, wgl_ref, wgate_ref, bgate_ref,
                 q_ref, k_ref, v_ref, qg_ref, kg_ref, vg_ref, rg_ref, la_ref):
    h = _rms(x_ref[...], g_ref[...]) * (1.0 + sc_ref[0]) + sh_ref[0]
    hb = h.astype(BF16)
    p1 = jnp.dot(hb, wsb_ref[...], preferred_element_type=F32)
    q_ref[...] = (p1[:, :SB_WIDTH] * (SB_HEAD_DIM ** -0.5)).astype(BF16)
    k_ref[...] = p1[:, SB_WIDTH:2 * SB_WIDTH]
    v_ref[...] = p1[:, 2 * SB_WIDTH:3 * SB_WIDTH]
    p2 = jnp.dot(hb, wgl_ref[...], preferred_element_type=F32)
    o = 0
    qg_ref[...] = p2[:, o:o + GLA_QK_WIDTH] * (GLA_KEY_DIM ** -0.5)
    o += GLA_QK_WIDTH
    kg_ref[...] = p2[:, o:o + GLA_QK_WIDTH]
    o += GLA_QK_WIDTH
    vg_ref[...] = p2[:, o:o + GLA_V_WIDTH].astype(BF16)
    o += GLA_V_WIDTH
    rg_ref[...] = p2[:, o:o + GLA_V_WIDTH]
    o += GLA_V_WIDTH
    a_low = p2[:, o:o + LANES]
    xg = jnp.dot(a_low.astype(BF16), wgate_ref[...], preferred_element_type=F32) + bgate_ref[...]
    la_ref[...] = (jnp.minimum(xg, 0.0) - jnp.log(1.0 + jnp.exp(-jnp.abs(xg)))) * (1.0 / GLA_GATE_TEMP)


def _mod_spec(mod, tm, rows_per_group):
    if mod.shape[1] == 1:
        tiles_per_group = rows_per_group // tm
        return pl.BlockSpec((1, 1, mod.shape[2]), lambda i: (i // tiles_per_group, 0, 0))
    return pl.BlockSpec((1, tm, mod.shape[2]), lambda i: (0, i, 0))


def _inproj(x2d, sh, sc, g_mix, w_sb, w_gl, w_gate, b_gate, tm, rows_per_group):
    m, d = x2d.shape
    row = lambda w: pl.BlockSpec((tm, w), lambda i: (i, 0))
    full = lambda a: pl.BlockSpec(a.shape, lambda i: (0,) * a.ndim)
    outs = [(SB_WIDTH, BF16), (SB_WIDTH, F32), (SB_WIDTH, F32), (GLA_QK_WIDTH, F32),
            (GLA_QK_WIDTH, F32), (GLA_V_WIDTH, BF16), (GLA_V_WIDTH, F32), (GLA_QK_WIDTH, F32)]
    return pl.pallas_call(
        _inproj_body,
        grid=(m // tm,),
        in_specs=[row(d), _mod_spec(sh, tm, rows_per_group), _mod_spec(sc, tm, rows_per_group),
                  full(g_mix), full(w_sb), full(w_gl), full(w_gate), full(b_gate)],
        out_specs=[row(w) for w, _ in outs],
        out_shape=[jax.ShapeDtypeStruct((m, w), dt) for w, dt in outs],
        compiler_params=pltpu.CompilerParams(dimension_semantics=("parallel",),
                                             vmem_limit_bytes=VMEM_LIMIT),
        name="in_proj",
    )(x2d, sh, sc, g_mix, w_sb, w_gl, w_gate, b_gate)


def _sb_tile(qh, kb, vb, u, bias, carry, acc, causal):
    z = lax.dot_general(qh, kb, _NT, preferred_element_type=F32) + bias
    sp = _softplus(z)
    if causal is not None:
        sp = jnp.where(causal, sp, 0.0)
    a = jnp.dot(sp.astype(BF16), u, preferred_element_type=F32)
    n_rep = a.shape[1] // LANES
    e = z - a - jnp.concatenate([carry] * n_rep, axis=1)
    w = jnp.exp(e)
    if causal is not None:
        w = jnp.where(causal, w, 0.0)
    acc = acc + jnp.dot(w.astype(BF16), vb, preferred_element_type=F32)
    carry = carry + jnp.broadcast_to(a[:, 0:1], carry.shape)
    return carry, acc


def _sb_prompt_body(bias_ref, q_ref, k_ref, v_ref, u_ref, o_ref):
    bq = q_ref.shape[0]
    p = pl.program_id(1)
    i = pl.program_id(2)
    q = q_ref[...]
    u = u_ref[...]
    lane = lax.broadcasted_iota(jnp.int32, (bq, LANES), 1)
    first_head = lane < SB_HEAD_DIM
    qh = (jnp.where(first_head, q, jnp.zeros_like(q)), jnp.where(first_head, jnp.zeros_like(q), q))
    bias = (bias_ref[2 * p], bias_ref[2 * p + 1])
    row = lax.broadcasted_iota(jnp.int32, (bq, bq), 0)
    col = lax.broadcasted_iota(jnp.int32, (bq, bq), 1)
    causal = col < row

    def step(j, state, mask):
        start = pl.multiple_of(j * bq, bq)
        kb = k_ref[pl.ds(start, bq), :].astype(BF16)
        vb = v_ref[pl.ds(start, bq), :].astype(BF16)
        c0, a0, c1, a1 = state
        c0, a0 = _sb_tile(qh[0], kb, vb, u, bias[0], c0, a0, mask)
        c1, a1 = _sb_tile(qh[1], kb, vb, u, bias[1], c1, a1, mask)
        return c0, a0, c1, a1

    zero = jnp.zeros((bq, LANES), F32)
    state = step(i, (zero, zero, zero, zero), causal)
    state = lax.fori_loop(0, i, lambda t, s: step(i - 1 - t, s, None), state)
    o_ref[...] = jnp.where(first_head, state[1], state[3]).astype(BF16)


def _suffix_ones(n):
    r = np.arange(n)
    return jnp.asarray((r[:, None] >= r[None, :]).astype(np.float32), dtype=BF16)


def _sb_prompt(q, k, v, b_sb, bq=SB_BQ):
    b, t, w = q.shape
    bq = min(bq, t)
    pairs = w // LANES
    return pl.pallas_call(
        _sb_prompt_body,
        grid=(b, pairs, t // bq),
        in_specs=[pl.BlockSpec(memory_space=pltpu.SMEM),
                  pl.BlockSpec((None, bq, LANES), lambda bb, p, i: (bb, i, p)),
                  pl.BlockSpec((None, t, LANES), lambda bb, p, i: (bb, 0, p)),
                  pl.BlockSpec((None, t, LANES), lambda bb, p, i: (bb, 0, p)),
                  pl.BlockSpec((bq, bq), lambda bb, p, i: (0, 0))],
        out_specs=pl.BlockSpec((None, bq, LANES), lambda bb, p, i: (bb, i, p)),
        out_shape=jax.ShapeDtypeStruct((b, t, w), BF16),
        compiler_params=pltpu.CompilerParams(
            dimension_semantics=("parallel", "parallel", "arbitrary"),
            vmem_limit_bytes=VMEM_LIMIT),
        name="sb_prompt",
    )(b_sb, q, k, v, _suffix_ones(bq))


def _sb_decode_body(pt_ref, bias_ref, q_ref, kn_ref, vn_ref, *rest):
    del pt_ref
    npg = DEC_PAGES_PER_STEP
    k_pages = rest[:npg]
    v_pages = rest[npg:2 * npg]
    u_ref, o_ref, carry_sc, acc_sc = rest[2 * npg:]
    g = pl.program_id(1)
    n_tok = q_ref.shape[1]
    rows = SB_HEADS * n_tok
    page = k_pages[0].shape[1]

    row = lax.broadcasted_iota(jnp.int32, (rows, SB_WIDTH), 0)
    lane = lax.broadcasted_iota(jnp.int32, (rows, SB_WIDTH), 1)
    own = (row // n_tok) == (lane // SB_HEAD_DIM)
    q_all = jnp.concatenate([q_ref[0].astype(F32)] * SB_HEADS, axis=0)
    qbd = jnp.where(own, q_all, 0.0).astype(BF16)
    row_p = lax.broadcasted_iota(jnp.int32, (rows, page), 0)
    col_p = lax.broadcasted_iota(jnp.int32, (rows, page), 1)
    bias = jnp.zeros((rows, page), F32)
    for h in range(SB_HEADS):
        bias = jnp.where(row_p // n_tok == h, bias_ref[h], bias)
    u = u_ref[...]

    @pl.when(g == 0)
    def _():
        pad = jnp.zeros((page - n_tok, SB_WIDTH), F32)
        kb = jnp.concatenate([kn_ref[0], pad], axis=0).astype(BF16)
        vb = jnp.concatenate([vn_ref[0], pad], axis=0).astype(BF16)
        causal = col_p < (row_p % n_tok)
        zero_c = jnp.zeros((rows, LANES), F32)
        zero_a = jnp.zeros((rows, SB_WIDTH), F32)
        c, a = _sb_tile(qbd, kb, vb, u, bias, zero_c, zero_a, causal)
        carry_sc[...] = c
        acc_sc[...] = a

    c = carry_sc[...]
    a = acc_sc[...]
    for n in reversed(range(npg)):
        kb = k_pages[n][0].astype(BF16)
        vb = v_pages[n][0].astype(BF16)
        c, a = _sb_tile(qbd, kb, vb, u, bias, c, a, None)
    carry_sc[...] = c
    acc_sc[...] = a

    @pl.when(g == pl.num_programs(1) - 1)
    def _():
        am = jnp.where(own, a, 0.0)
        out = am[0:n_tok]
        for h in range(1, SB_HEADS):
            out = out + am[h * n_tok:(h + 1) * n_tok]
        o_ref[0] = out.astype(BF16)


def _sb_decode(q, k_new, v_new, cache_k, cache_v, page_table, b_sb):
    nb, n_tok, w = q.shape
    n_pages = page_table.shape[1]
    page = cache_k.shape[1]
    npg = DEC_PAGES_PER_STEP
    groups = n_pages // npg
    rows = SB_HEADS * n_tok

    def page_map(n):
        return lambda b, g, pt: (pt[b, (groups - 1 - g) * npg + n], 0, 0)

    tok_spec = pl.BlockSpec((1, n_tok, w), lambda b, g, pt: (b, 0, 0))
    page_specs = [pl.BlockSpec((1, page, w), page_map(n)) for n in range(npg)]
    grid_spec = pltpu.PrefetchScalarGridSpec(
        num_scalar_prefetch=1,
        grid=(nb, groups),
        in_specs=[pl.BlockSpec(memory_space=pltpu.SMEM), tok_spec, tok_spec, tok_spec]
        + page_specs + page_specs
        + [pl.BlockSpec((page, page), lambda b, g, pt: (0, 0))],
        out_specs=tok_spec,
        scratch_shapes=[pltpu.VMEM((rows, LANES), F32), pltpu.VMEM((rows, w), F32)],
    )
    return pl.pallas_call(
        _sb_decode_body,
        grid_spec=grid_spec,
        out_shape=jax.ShapeDtypeStruct((nb, n_tok, w), BF16),
        compiler_params=pltpu.CompilerParams(
            dimension_semantics=("parallel", "arbitrary"), vmem_limit_bytes=VMEM_LIMIT),
        name="sb_decode",
    )(page_table, b_sb, q, k_new, v_new, *([cache_k] * npg), *([cache_v] * npg), _suffix_ones(page))


def _gla_consts(tb, chunk):
    i = np.arange(tb)[:, None]
    t = np.arange(tb)[None, :]
    mats = []
    lvl = np.zeros((tb, tb), np.float32)
    code = 1
    s = tb
    while s > chunk:
        mid = (i // s) * s + s // 2
        upper = i >= mid
        mats.append(np.where(upper & (t >= mid) & (t <= i), 1.0, 0.0)
                    + np.where(~upper & (t > i) & (t < mid), -1.0, 0.0))
        mid_t = (t // s) * s + s // 2
        lvl = np.where((i // s == t // s) & upper & (t < mid_t), code, lvl)
        code += 1
        s //= 2
    mats.append(np.where((t // chunk == i // chunk) & (t <= i), 1.0, 0.0))
    lvl = np.where((i // chunk == t // chunk) & (t <= i), code, lvl)
    mats.append(np.where(t <= i, 1.0, 0.0))
    stack = np.concatenate(mats, axis=0).astype(np.float32)
    return jnp.asarray(stack, dtype=BF16), jnp.asarray(lvl, dtype=F32), code


def _gla_body(q_ref, k_ref, v_ref, r_ref, la_ref, s0_ref, g_ref, m_ref, lvl_ref,
              o_ref, st_ref, s_sc, *, n_levels):
    tb = lvl_ref.shape[0]
    n_tok = q_ref.shape[1]
    blk = pl.program_id(1)

    @pl.when(blk == 0)
    def _():
        s_sc[...] = s0_ref[0]

    def load(ref, lo, width):
        x = ref[0, :, lo:lo + width]
        if n_tok < tb:
            x = jnp.concatenate([x.astype(F32), jnp.zeros((tb - n_tok, width), F32)], axis=0).astype(x.dtype)
        return x

    lvl = lvl_ref[...]
    lane = lax.broadcasted_iota(jnp.int32, (tb, LANES), 1)
    first_head = lane < GLA_KEY_DIM
    row_s = lax.broadcasted_iota(jnp.int32, (2 * GLA_VAL_DIM, LANES), 0)
    lane_s = lax.broadcasted_iota(jnp.int32, (2 * GLA_VAL_DIM, LANES), 1)
    own_state = (row_s < GLA_VAL_DIM) == (lane_s < GLA_KEY_DIM)
    gain = g_ref[...]

    for p in range(GLA_HEADS // 2):
        la = load(la_ref, p * LANES, LANES)
        hi = la.astype(BF16)
        lo = (la - hi.astype(F32)).astype(BF16)
        res = jnp.dot(m_ref[...], jnp.concatenate([hi, lo], axis=1), preferred_element_type=F32)
        dall = res[:, :LANES] + res[:, LANES:]
        d = [dall[n * tb:(n + 1) * tb] for n in range(n_levels + 1)]
        bc = d[n_levels]
        e_tot = bc[tb - 1:tb, :]
        q = load(q_ref, p * LANES, LANES)
        k = load(k_ref, p * LANES, LANES)
        qs, ks = [], []
        for n in range(n_levels - 1):
            x = jnp.exp(-jnp.abs(d[n]))
            qs.append(q * x)
            ks.append(k * x)
        qs.append(q * jnp.exp(d[n_levels - 1]))
        ks.append(k * jnp.exp(-d[n_levels - 1]))
        q_hat = (q * jnp.exp(bc)).astype(BF16)
        k_hat = (k * jnp.exp(e_tot - bc)).astype(BF16)
        st = s_sc[p]
        o_inter = lax.dot_general(q_hat, st.astype(BF16), _NT, preferred_element_type=F32)
        o_heads = []
        for hh in range(2):
            head = first_head if hh == 0 else jnp.logical_not(first_head)
            sc = jnp.zeros((tb, tb), F32)
            for n in range(n_levels):
                s_n = lax.dot_general(jnp.where(head, qs[n], 0.0).astype(BF16), ks[n].astype(BF16),
                                      _NT, preferred_element_type=F32)
                sc = jnp.where(lvl == float(n + 1), s_n, sc)
            vh = load(v_ref, (2 * p + hh) * GLA_VAL_DIM, GLA_VAL_DIM)
            o_heads.append(jnp.dot(sc.astype(BF16), vh, preferred_element_type=F32))
        o = jnp.concatenate(o_heads, axis=1) + o_inter
        v_pair = load(v_ref, 2 * p * GLA_VAL_DIM, 2 * GLA_VAL_DIM)
        ds_t = lax.dot_general(v_pair, k_hat, _TN, preferred_element_type=F32)
        st_new = st * jnp.exp(e_tot) + jnp.where(own_state, ds_t, 0.0)
        s_sc[p] = st_new
        st_ref[0, p] = st_new
        for hh in range(2):
            h = 2 * p + hh
            oh = _rms(o[:n_tok, hh * GLA_VAL_DIM:(hh + 1) * GLA_VAL_DIM], gain)
            r = r_ref[0, :, h * GLA_VAL_DIM:(h + 1) * GLA_VAL_DIM]
            o_ref[0, :, h * GLA_VAL_DIM:(h + 1) * GLA_VAL_DIM] = (oh * _silu(r)).astype(BF16)


def _gla(qg, kg, vg, rg, la, s0_t, g_out):
    b, t, _ = qg.shape
    tb = GLA_TB
    tok = min(t, tb)
    nblk = max(t // tb, 1)
    m_stack, lvl, n_levels = _gla_consts(tb, GLA_CHUNK if t % GLA_CHUNK == 0 else t)
    blk = lambda w: pl.BlockSpec((1, tok, w), lambda bb, j: (bb, j, 0))
    full = lambda a: pl.BlockSpec(a.shape, lambda bb, j: (0,) * a.ndim)
    st_spec = pl.BlockSpec((1,) + s0_t.shape[1:], lambda bb, j: (bb, 0, 0, 0))
    return pl.pallas_call(
        functools.partial(_gla_body, n_levels=n_levels),
        grid=(b, nblk),
        in_specs=[blk(GLA_QK_WIDTH), blk(GLA_QK_WIDTH), blk(GLA_V_WIDTH), blk(GLA_V_WIDTH),
                  blk(GLA_QK_WIDTH), st_spec, full(g_out), full(m_stack), full(lvl)],
        out_specs=[blk(GLA_V_WIDTH), st_spec],
        out_shape=[jax.ShapeDtypeStruct((b, t, GLA_V_WIDTH), BF16),
                   jax.ShapeDtypeStruct(s0_t.shape, F32)],
        scratch_shapes=[pltpu.VMEM(s0_t.shape[1:], F32)],
        compiler_params=pltpu.CompilerParams(
            dimension_semantics=("parallel", "arbitrary"), vmem_limit_bytes=VMEM_LIMIT),
        name="gla",
    )(qg, kg, vg, rg, la, s0_t, g_out, m_stack, lvl)


def _state_to_t(s):
    b = s.shape[0]
    st = jnp.swapaxes(s, 2, 3).reshape(b, GLA_HEADS // 2, 2, GLA_VAL_DIM, GLA_KEY_DIM)
    z = jnp.zeros_like(st[:, :, 0])
    top = jnp.concatenate([st[:, :, 0], z], axis=-1)
    bot = jnp.concatenate([z, st[:, :, 1]], axis=-1)
    return jnp.concatenate([top, bot], axis=2)


def _state_from_t(st):
    b = st.shape[0]
    h0 = st[:, :, :GLA_VAL_DIM, :GLA_KEY_DIM]
    h1 = st[:, :, GLA_VAL_DIM:, GLA_KEY_DIM:]
    s = jnp.stack([h0, h1], axis=2).reshape(b, GLA_HEADS, GLA_VAL_DIM, GLA_KEY_DIM)
    return jnp.swapaxes(s, 2, 3)


def _post_body(x_ref, osb_ref, og_ref, ga1_ref, sh2_ref, sc2_ref, ga2_ref, gffn_ref, gfin_ref,
               wo_ref, wup_ref, wdn_ref, y_ref):
    mix = (jnp.dot(osb_ref[...], wo_ref[:SB_WIDTH, :], preferred_element_type=F32)
           + jnp.dot(og_ref[...], wo_ref[SB_WIDTH:, :], preferred_element_type=F32))
    x1 = x_ref[...] + ga1_ref[0] * mix
    h2 = (_rms(x1, gffn_ref[...]) * (1.0 + sc2_ref[0]) + sh2_ref[0]).astype(BF16)
    ff = jnp.zeros_like(x1)
    for c in range(D_FF // FF_CHUNK):
        u = jnp.dot(h2, wup_ref[:, c * FF_CHUNK:(c + 1) * FF_CHUNK], preferred_element_type=F32)
        a = jnp.square(jnp.maximum(u, 0.0)).astype(BF16)
        ff = ff + jnp.dot(a, wdn_ref[c * FF_CHUNK:(c + 1) * FF_CHUNK, :], preferred_element_type=F32)
    x2 = x1 + ga2_ref[0] * ff
    y_ref[...] = _rms(x2, gfin_ref[...])


def _post(x2d, o_sb, o_g, ga1, sh2, sc2, ga2, g_ffn, g_final, w_out, w_up, w_down, tm, rows_per_group):
    m, d = x2d.shape
    row = lambda w: pl.BlockSpec((tm, w), lambda i: (i, 0))
    full = lambda a: pl.BlockSpec(a.shape, lambda i: (0,) * a.ndim)
    resident = lambda a: pl.BlockSpec(a.shape, lambda i: (0,) * a.ndim, pipeline_mode=pl.Buffered(1))
    mod = lambda a: _mod_spec(a, tm, rows_per_group)
    return pl.pallas_call(
        _post_body,
        grid=(m // tm,),
        in_specs=[row(d), row(SB_WIDTH), row(GLA_V_WIDTH), mod(ga1), mod(sh2), mod(sc2), mod(ga2),
                  full(g_ffn), full(g_final), resident(w_out), resident(w_up), resident(w_down)],
        out_specs=row(d),
        out_shape=jax.ShapeDtypeStruct((m, d), F32),
        compiler_params=pltpu.CompilerParams(dimension_semantics=("parallel",),
                                             vmem_limit_bytes=VMEM_LIMIT),
        name="post_ffn",
    )(x2d, o_sb, o_g, ga1, sh2, sc2, ga2, g_ffn, g_final, w_out, w_up, w_down)


def _prep_weights(w_in, w_gla_gate, b_gla_gate, w_out, w_up, w_down):
    n_sb = 3 * SB_WIDTH
    n_gl = 2 * GLA_QK_WIDTH + 2 * GLA_V_WIDTH
    w_sb = w_in[:, :n_sb].astype(BF16)
    a_pad = jnp.pad(w_in[:, n_sb + n_gl:], ((0, 0), (0, LANES - GLA_GATE_RANK)))
    w_gl = jnp.concatenate([w_in[:, n_sb:n_sb + n_gl], a_pad], axis=1).astype(BF16)
    w_gate = jnp.pad(w_gla_gate, ((0, LANES - GLA_GATE_RANK), (0, 0))).astype(BF16)
    return (w_sb, w_gl, w_gate, b_gla_gate.reshape(1, -1),
            w_out.astype(BF16), w_up.astype(BF16), w_down.astype(BF16))


def _layer(x, mods, weights, norms, sb_fn, s0_t, tm):
    g_rows, t, d = x.shape
    sh1, sc1, ga1, sh2, sc2, ga2 = mods
    w_sb, w_gl, w_gate, b_gate, w_out, w_up, w_down = weights
    g_mix, g_gla_out, g_ffn, g_final = norms
    x2d = x.reshape(g_rows * t, d)
    rows_per_group = t if sh1.shape[1] == 1 else g_rows * t
    q, k, v, qg, kg, vg, rg, la = _inproj(x2d, sh1, sc1, g_mix, w_sb, w_gl, w_gate, b_gate,
                                          tm, rows_per_group)
    r3 = lambda a: a.reshape(g_rows, t, a.shape[-1])
    o_sb = sb_fn(r3(q), r3(k), r3(v))
    o_g, st = _gla(r3(qg), r3(kg), r3(vg), r3(rg), r3(la), s0_t, g_gla_out)
    y = _post(x2d, o_sb.reshape(g_rows * t, -1), o_g.reshape(g_rows * t, -1), ga1, sh2, sc2, ga2,
              g_ffn, g_final, w_out, w_up, w_down, tm, rows_per_group)
    return (y.reshape(g_rows, t, d), k.reshape(g_rows, t, SB_HEADS, SB_HEAD_DIM),
            v.reshape(g_rows, t, SB_HEADS, SB_HEAD_DIM), _state_from_t(st))


def kernel(x_prompt, x_sample, c_prompt, c_sample, cache_k, cache_v, state_gla, page_table, w_ada, b_ada, g_mix, w_in, b_sb, w_gla_gate, b_gla_gate, g_gla_out, w_out, g_ffn, w_up, w_down, g_final):
    depth = w_ada.shape[0]
    assert depth == 1, "final-norm fusion assumes a single layer"
    nb, t, d = x_prompt.shape
    nd, td, _ = x_sample.shape
    l = 0

    c_all = jnp.concatenate([c_prompt, c_sample], axis=0)
    pad_rows = (-c_all.shape[0]) % SUBLANES
    mod = _ada(jnp.pad(c_all, ((0, pad_rows), (0, 0))), w_ada[l], b_ada[l])
    mods_p = [m[:nb].reshape(nb, 1, d) for m in jnp.split(mod, 6, axis=-1)]
    mods_s = [jnp.repeat(m[nb:nb + nd], td, axis=0).reshape(1, nd * td, d)
              for m in jnp.split(mod, 6, axis=-1)]

    weights = _prep_weights(w_in[l], w_gla_gate[l], b_gla_gate[l], w_out[l], w_up[l], w_down[l])
    norms = (g_mix[l].reshape(1, d), g_gla_out[l].reshape(1, -1), g_ffn[l].reshape(1, d),
             g_final.reshape(1, d))

    sb_p = lambda q, k, v: _sb_prompt(q, k, v, b_sb[l])
    zeros_t = jnp.zeros((nb, GLA_HEADS // 2, 2 * GLA_VAL_DIM, 2 * GLA_KEY_DIM), F32)
    y_p, k_p, v_p, s_p = _layer(x_prompt, mods_p, weights, norms, sb_p, zeros_t, min(ROW_TILE, t))

    page = cache_k.shape[2]
    ck = cache_k[l].reshape(cache_k.shape[1], page, SB_WIDTH)
    cv = cache_v[l].reshape(cache_v.shape[1], page, SB_WIDTH)
    sb_s = lambda q, k, v: _sb_decode(q, k, v, ck, cv, page_table, b_sb[l])
    y_s, k_s, v_s, s_s = _layer(x_sample, mods_s, weights, norms, sb_s, _state_to_t(state_gla[l]),
                                nd * td)

    return (y_p, y_s, k_p[None], v_p[None], s_p[None], k_s[None], v_s[None], s_s[None])
```

```python
import functools

import numpy as np
import jax
import jax.numpy as jnp
from jax import lax
from jax.experimental import pallas as pl
from jax.experimental.pallas import tpu as pltpu

F32 = jnp.float32
BF16 = jnp.bfloat16

D_MODEL = 1024
SB_HEADS = 8
SB_HEAD_DIM = 64
SB_WIDTH = SB_HEADS * SB_HEAD_DIM
GLA_HEADS = 4
GLA_KEY_DIM = 64
GLA_VAL_DIM = 128
GLA_QK_WIDTH = GLA_HEADS * GLA_KEY_DIM
GLA_V_WIDTH = GLA_HEADS * GLA_VAL_DIM
GLA_GATE_RANK = 16
GLA_GATE_TEMP = 16.0
GLA_CHUNK = 16
D_FF = 4 * D_MODEL
NORM_EPS = 1e-6

LANES = 128
SUBLANES = 8
VMEM_LIMIT = 56 * 1024 * 1024

SB_BQ = 256
GLA_TB = 128
DEC_PAGES_PER_STEP = 16
ROW_TILE = 512
ADA_TN = 1024
FF_CHUNK = 1024
SB_MASKED = -1e30

_NT = (((1,), (1,)), ((), ()))
_TN = (((0,), (0,)), ((), ()))


LOG2E = float(np.log2(np.e))


def _softplus2(z2):
    return jnp.maximum(z2, 0.0) + jnp.log(1.0 + jnp.exp2(-jnp.abs(z2))) * LOG2E


def _silu(x):
    return x * jax.nn.sigmoid(x)


def _rms(x, gain):
    ms = jnp.mean(x * x, axis=-1, keepdims=True)
    return x * lax.rsqrt(ms + NORM_EPS) * gain


def _ada_body(c_ref, w_ref, b_ref, o_ref):
    s = _silu(c_ref[...])
    o_ref[...] = jnp.dot(s.astype(BF16), w_ref[...].astype(BF16),
                         preferred_element_type=F32) + b_ref[...]


def _ada(c, w_ada, b_ada):
    rows, d = c.shape
    n = w_ada.shape[1]
    return pl.pallas_call(
        _ada_body,
        grid=(n // ADA_TN,),
        in_specs=[pl.BlockSpec((rows, d), lambda j: (0, 0)),
                  pl.BlockSpec((d, ADA_TN), lambda j: (0, j)),
                  pl.BlockSpec((1, ADA_TN), lambda j: (0, j))],
        out_specs=pl.BlockSpec((rows, ADA_TN), lambda j: (0, j)),
        out_shape=jax.ShapeDtypeStruct((rows, n), F32),
        name="ada_mod",
    )(c, w_ada, b_ada.reshape(1, n))


def _inproj_body(x_ref, sh_ref, sc_ref, g_ref, wsb_ref, wgl_ref, wgate_ref, bgate_ref,
                 q_ref, k_ref, v_ref, qg_ref, kg_ref, vg_ref, rg_ref, la_ref):
    h = _rms(x_ref[...], g_ref[...]) * (1.0 + sc_ref[0]) + sh_ref[0]
    hb = h.astype(BF16)
    p1 = jnp.dot(hb, wsb_ref[...], preferred_element_type=F32)
    q_ref[...] = (p1[:, :SB_WIDTH] * (LOG2E * SB_HEAD_DIM ** -0.5)).astype(BF16)
    k_ref[...] = p1[:, SB_WIDTH:2 * SB_WIDTH]
    v_ref[...] = p1[:, 2 * SB_WIDTH:3 * SB_WIDTH]
    p2 = jnp.dot(hb, wgl_ref[...], preferred_element_type=F32)
    o = 0
    qg_ref[...] = p2[:, o:o + GLA_QK_WIDTH] * (GLA_KEY_DIM ** -0.5)
    o += GLA_QK_WIDTH
    kg_ref[...] = p2[:, o:o + GLA_QK_WIDTH]
    o += GLA_QK_WIDTH
    vg_ref[...] = p2[:, o:o + GLA_V_WIDTH].astype(BF16)
    o += GLA_V_WIDTH
    rg_ref[...] = p2[:, o:o + GLA_V_WIDTH]
    o += GLA_V_WIDTH
    a_low = p2[:, o:o + LANES]
    xg = jnp.dot(a_low.astype(BF16), wgate_ref[...], preferred_element_type=F32) + bgate_ref[...]
    la_ref[...] = (jnp.minimum(xg, 0.0) - jnp.log(1.0 + jnp.exp(-jnp.abs(xg)))) * (1.0 / GLA_GATE_TEMP)


def _mod_spec(mod, tm, rows_per_group):
    if mod.shape[1] == 1:
        tiles_per_group = rows_per_group // tm
        return pl.BlockSpec((1, 1, mod.shape[2]), lambda i: (i // tiles_per_group, 0, 0))
    return pl.BlockSpec((1, tm, mod.shape[2]), lambda i: (0, i, 0))


def _inproj(x2d, sh, sc, g_mix, w_sb, w_gl, w_gate, b_gate, tm, rows_per_group):
    m, d = x2d.shape
    row = lambda w: pl.BlockSpec((tm, w), lambda i: (i, 0))
    full = lambda a: pl.BlockSpec(a.shape, lambda i: (0,) * a.ndim)
    outs = [(SB_WIDTH, BF16), (SB_WIDTH, F32), (SB_WIDTH, F32), (GLA_QK_WIDTH, F32),
            (GLA_QK_WIDTH, F32), (GLA_V_WIDTH, BF16), (GLA_V_WIDTH, F32), (GLA_QK_WIDTH, F32)]
    return pl.pallas_call(
        _inproj_body,
        grid=(m // tm,),
        in_specs=[row(d), _mod_spec(sh, tm, rows_per_group), _mod_spec(sc, tm, rows_per_group),
                  full(g_mix), full(w_sb), full(w_gl), full(w_gate), full(b_gate)],
        out_specs=[row(w) for w, _ in outs],
        out_shape=[jax.ShapeDtypeStruct((m, w), dt) for w, dt in outs],
        compiler_params=pltpu.CompilerParams(dimension_semantics=("parallel",),
                                             vmem_limit_bytes=VMEM_LIMIT),
        name="in_proj",
    )(x2d, sh, sc, g_mix, w_sb, w_gl, w_gate, b_gate)


def _sb_prompt_body(bias_ref, q_ref, k_ref, v_ref, u_ref, o_ref,
                    mask_sc, z_sc, e_sc, cf_sc, carry_sc, acc_sc):
    bq = q_ref.shape[0]
    p = pl.program_id(1)
    i = pl.program_id(2)
    n = i + 1

    @pl.when(i == 0)
    def _():
        row = lax.broadcasted_iota(jnp.int32, (bq, bq), 0)
        col = lax.broadcasted_iota(jnp.int32, (bq, bq), 1)
        for hh in range(2):
            b = bias_ref[2 * p + hh] * LOG2E
            mask_sc[hh, 0] = jnp.where(col < row, b, SB_MASKED)
            mask_sc[hh, 1] = jnp.full((bq, bq), b, F32)
            mask_sc[hh, 2] = jnp.full((bq, bq), SB_MASKED, F32)

    q = q_ref[...]
    lane = lax.broadcasted_iota(jnp.int32, (bq, LANES), 1)
    first_head = lane < SB_HEAD_DIM
    qh = (jnp.where(first_head, q, jnp.zeros_like(q)), jnp.where(first_head, jnp.zeros_like(q), q))
    dead = jnp.full((bq, bq), SB_MASKED, F32)
    for hh in range(2):
        z_sc[1, hh] = dead
        e_sc[0, hh] = dead
        e_sc[1, hh] = dead
        cf_sc[0, hh] = jnp.zeros((bq, LANES), F32)
        cf_sc[1, hh] = jnp.zeros((bq, LANES), F32)
        carry_sc[hh] = jnp.zeros((bq, LANES), F32)
        acc_sc[hh] = jnp.zeros((bq, LANES), F32)

    def tile_start(t):
        j = i - jnp.clip(t, 0, n - 1)
        return pl.multiple_of(j * bq, bq)

    def iteration(t, slot):
        vb = v_ref[pl.ds(tile_start(t - 2), bq), :].astype(BF16)
        for hh in range(2):
            w = jnp.exp2(e_sc[slot, hh])
            acc_sc[hh] += cf_sc[slot, hh] * jnp.dot(w.astype(BF16), vb, preferred_element_type=F32)
        for hh in range(2):
            z = z_sc[1 - slot, hh]
            c = carry_sc[hh]
            a = jnp.dot(_softplus2(z).astype(BF16), u_ref[...], preferred_element_type=F32)
            e_sc[1 - slot, hh] = z - a
            cf_sc[1 - slot, hh] = jnp.exp2(-c)
            carry_sc[hh] = c + jnp.broadcast_to(a[:, 0:1], c.shape)
        kb = k_ref[pl.ds(tile_start(t), bq), :].astype(BF16)
        m = jnp.where(t == 0, 0, jnp.where(t < n, 1, 2))
        for hh in range(2):
            z_sc[slot, hh] = (lax.dot_general(qh[hh], kb, _NT, preferred_element_type=F32)
                              + mask_sc[hh, m])

    def body(tt, _):
        iteration(2 * tt, 0)
        iteration(2 * tt + 1, 1)
        return 0

    lax.fori_loop(0, (n + 3) // 2, body, 0)
    o_ref[...] = jnp.where(first_head, acc_sc[0], acc_sc[1]).astype(BF16)


def _suffix_ones(n):
    r = np.arange(n)
    return jnp.asarray((r[:, None] >= r[None, :]).astype(np.float32), dtype=BF16)


def _sb_prompt(q, k, v, b_sb, bq=SB_BQ):
    b, t, w = q.shape
    bq = min(bq, t)
    pairs = w // LANES
    return pl.pallas_call(
        _sb_prompt_body,
        grid=(b, pairs, t // bq),
        in_specs=[pl.BlockSpec(memory_space=pltpu.SMEM),
                  pl.BlockSpec((None, bq, LANES), lambda bb, p, i: (bb, i, p)),
                  pl.BlockSpec((None, t, LANES), lambda bb, p, i: (bb, 0, p)),
                  pl.BlockSpec((None, t, LANES), lambda bb, p, i: (bb, 0, p)),
                  pl.BlockSpec((bq, bq), lambda bb, p, i: (0, 0))],
        out_specs=pl.BlockSpec((None, bq, LANES), lambda bb, p, i: (bb, i, p)),
        out_shape=jax.ShapeDtypeStruct((b, t, w), BF16),
        scratch_shapes=[pltpu.VMEM((2, 3, bq, bq), F32),
                        pltpu.VMEM((2, 2, bq, bq), F32),
                        pltpu.VMEM((2, 2, bq, bq), F32),
                        pltpu.VMEM((2, 2, bq, LANES), F32),
                        pltpu.VMEM((2, bq, LANES), F32),
                        pltpu.VMEM((2, bq, LANES), F32)],
        compiler_params=pltpu.CompilerParams(
            dimension_semantics=("parallel", "parallel", "arbitrary"),
            vmem_limit_bytes=VMEM_LIMIT),
        name="sb_prompt",
    )(b_sb, q, k, v, _suffix_ones(bq))


def _sb_decode_tile(qh, kh, vh, u, bias, carry, acc, causal):
    n_tok = qh[0].shape[0]
    z = jnp.concatenate([lax.dot_general(qh[h], kh[h], _NT, preferred_element_type=F32)
                         for h in range(SB_HEADS)], axis=0) + bias
    sp = _softplus2(z)
    if causal is not None:
        sp = jnp.where(causal, sp, 0.0)
    a = jnp.dot(sp.astype(BF16), u, preferred_element_type=F32)
    w = jnp.exp2(z - a - carry)
    if causal is not None:
        w = jnp.where(causal, w, 0.0)
    pv = jnp.concatenate([jnp.dot(w[h * n_tok:(h + 1) * n_tok].astype(BF16), vh[h],
                                  preferred_element_type=F32) for h in range(SB_HEADS)], axis=0)
    return carry + jnp.broadcast_to(a[:, 0:1], carry.shape), acc + pv


def _page_heads(page_ref):
    page, heads, dim = page_ref.shape
    assert heads == SUBLANES
    groups = page // heads
    p4 = page_ref[...].reshape(groups, heads, heads, dim)
    x = [p4[:, v] for v in range(heads)]
    sub = lax.broadcasted_iota(jnp.int32, x[0].shape, 1)
    for dist in (4, 2, 1):
        keep = (sub & dist) == 0
        y = list(x)
        for v in range(heads):
            if v & dist == 0:
                a, b = x[v], x[v + dist]
                y[v] = jnp.where(keep, a, pltpu.roll(b, dist, axis=1))
                y[v + dist] = jnp.where(keep, pltpu.roll(a, heads - dist, axis=1), b)
        x = y
    return [x[h].reshape(page, dim).astype(BF16) for h in range(heads)]


def _sb_decode_body(pt_ref, bias_ref, q_ref, kn_ref, vn_ref, *rest):
    del pt_ref
    npg = DEC_PAGES_PER_STEP
    k_pages = rest[:npg]
    v_pages = rest[npg:2 * npg]
    u_ref, o_ref, carry_sc, acc_sc = rest[2 * npg:]
    g = pl.program_id(1)
    n_tok = q_ref.shape[1]
    rows = SB_HEADS * n_tok
    page = k_pages[0].shape[0]
    heads = range(SB_HEADS)
    head_lanes = lambda x, h: x[:, h * SB_HEAD_DIM:(h + 1) * SB_HEAD_DIM]

    q = q_ref[0].astype(F32)
    qh = [head_lanes(q, h).astype(BF16) for h in heads]
    row_p = lax.broadcasted_iota(jnp.int32, (rows, page), 0)
    col_p = lax.broadcasted_iota(jnp.int32, (rows, page), 1)
    bias = jnp.zeros((rows, page), F32)
    for h in heads:
        bias = jnp.where(row_p // n_tok == h, bias_ref[h] * LOG2E, bias)
    u = u_ref[...]

    @pl.when(g == 0)
    def _():
        pad = jnp.zeros((page - n_tok, SB_HEAD_DIM), F32)
        kh = [jnp.concatenate([head_lanes(kn_ref[0], h), pad], axis=0).astype(BF16) for h in heads]
        vh = [jnp.concatenate([head_lanes(vn_ref[0], h), pad], axis=0).astype(BF16) for h in heads]
        causal = col_p < (row_p % n_tok)
        c, a = _sb_decode_tile(qh, kh, vh, u, bias, jnp.zeros((rows, LANES), F32),
                               jnp.zeros((rows, SB_HEAD_DIM), F32), causal)
        carry_sc[...] = c
        acc_sc[...] = a

    c = carry_sc[...]
    a = acc_sc[...]
    for n in reversed(range(npg)):
        c, a = _sb_decode_tile(qh, _page_heads(k_pages[n]), _page_heads(v_pages[n]), u, bias,
                               c, a, None)
    carry_sc[...] = c
    acc_sc[...] = a

    @pl.when(g == pl.num_programs(1) - 1)
    def _():
        o_ref[0] = a.astype(BF16)


def _sb_decode(q, k_new, v_new, cache_k, cache_v, page_table, b_sb):
    nb, n_tok, w = q.shape
    n_pages = page_table.shape[1]
    page = cache_k.shape[1]
    npg = DEC_PAGES_PER_STEP
    groups = n_pages // npg
    rows = SB_HEADS * n_tok

    def page_map(n):
        return lambda b, g, pt: (pt[b, (groups - 1 - g) * npg + n], 0, 0, 0)

    tok_spec = pl.BlockSpec((1, n_tok, w), lambda b, g, pt: (b, 0, 0))
    page_specs = [pl.BlockSpec((None, page, SB_HEADS, SB_HEAD_DIM), page_map(n)) for n in range(npg)]
    out_spec = pl.BlockSpec((1, rows, SB_HEAD_DIM), lambda b, g, pt: (b, 0, 0))
    grid_spec = pltpu.PrefetchScalarGridSpec(
        num_scalar_prefetch=1,
        grid=(nb, groups),
        in_specs=[pl.BlockSpec(memory_space=pltpu.SMEM), tok_spec, tok_spec, tok_spec]
        + page_specs + page_specs
        + [pl.BlockSpec((page, page), lambda b, g, pt: (0, 0))],
        out_specs=out_spec,
        scratch_shapes=[pltpu.VMEM((rows, LANES), F32), pltpu.VMEM((rows, SB_HEAD_DIM), F32)],
    )
    o = pl.pallas_call(
        _sb_decode_body,
        grid_spec=grid_spec,
        out_shape=jax.ShapeDtypeStruct((nb, rows, SB_HEAD_DIM), BF16),
        compiler_params=pltpu.CompilerParams(
            dimension_semantics=("parallel", "arbitrary"), vmem_limit_bytes=VMEM_LIMIT),
        name="sb_decode",
    )(page_table, b_sb, q, k_new, v_new, *([cache_k] * npg), *([cache_v] * npg), _suffix_ones(page))
    return o.reshape(nb, SB_HEADS, n_tok, SB_HEAD_DIM).transpose(0, 2, 1, 3).reshape(nb, n_tok, w)


def _gla_consts(tb, chunk):
    i = np.arange(tb)[:, None]
    t = np.arange(tb)[None, :]
    mats = []
    lvl = np.zeros((tb, tb), np.float32)
    code = 1
    s = tb
    while s > chunk:
        mid = (i // s) * s + s // 2
        upper = i >= mid
        mats.append(np.where(upper & (t >= mid) & (t <= i), 1.0, 0.0)
                    + np.where(~upper & (t > i) & (t < mid), -1.0, 0.0))
        mid_t = (t // s) * s + s // 2
        lvl = np.where((i // s == t // s) & upper & (t < mid_t), code, lvl)
        code += 1
        s //= 2
    mats.append(np.where((t // chunk == i // chunk) & (t <= i), 1.0, 0.0))
    lvl = np.where((i // chunk == t // chunk) & (t <= i), code, lvl)
    mats.append(np.where(t <= i, 1.0, 0.0))
    stack = np.concatenate(mats, axis=0).astype(np.float32)
    return jnp.asarray(stack, dtype=BF16), jnp.asarray(lvl, dtype=F32), code


def _gla_body(q_ref, k_ref, v_ref, r_ref, la_ref, s0_ref, g_ref, m_ref, lvl_ref,
              o_ref, st_ref, s_sc, *, n_levels):
    tb = lvl_ref.shape[0]
    n_tok = q_ref.shape[1]
    blk = pl.program_id(1)

    @pl.when(blk == 0)
    def _():
        s_sc[...] = s0_ref[0]

    def load(ref, lo, width):
        x = ref[0, :, lo:lo + width]
        if n_tok < tb:
            x = jnp.concatenate([x.astype(F32), jnp.zeros((tb - n_tok, width), F32)], axis=0).astype(x.dtype)
        return x

    lvl = lvl_ref[...]
    lane = lax.broadcasted_iota(jnp.int32, (tb, LANES), 1)
    first_head = lane < GLA_KEY_DIM
    row_s = lax.broadcasted_iota(jnp.int32, (2 * GLA_VAL_DIM, LANES), 0)
    lane_s = lax.broadcasted_iota(jnp.int32, (2 * GLA_VAL_DIM, LANES), 1)
    own_state = (row_s < GLA_VAL_DIM) == (lane_s < GLA_KEY_DIM)
    gain = g_ref[...]

    for p in range(GLA_HEADS // 2):
        la = load(la_ref, p * LANES, LANES)
        hi = la.astype(BF16)
        lo = (la - hi.astype(F32)).astype(BF16)
        res = jnp.dot(m_ref[...], jnp.concatenate([hi, lo], axis=1), preferred_element_type=F32)
        dall = res[:, :LANES] + res[:, LANES:]
        d = [dall[n * tb:(n + 1) * tb] for n in range(n_levels + 1)]
        bc = d[n_levels]
        e_tot = bc[tb - 1:tb, :]
        q = load(q_ref, p * LANES, LANES)
        k = load(k_ref, p * LANES, LANES)
        qs, ks = [], []
        for n in range(n_levels - 1):
            x = jnp.exp(-jnp.abs(d[n]))
            qs.append(q * x)
            ks.append(k * x)
        qs.append(q * jnp.exp(d[n_levels - 1]))
        ks.append(k * jnp.exp(-d[n_levels - 1]))
        q_hat = (q * jnp.exp(bc)).astype(BF16)
        k_hat = (k * jnp.exp(e_tot - bc)).astype(BF16)
        st = s_sc[p]
        o_inter = lax.dot_general(q_hat, st.astype(BF16), _NT, preferred_element_type=F32)
        o_heads = []
        for hh in range(2):
            head = first_head if hh == 0 else jnp.logical_not(first_head)
            sc = jnp.zeros((tb, tb), F32)
            for n in range(n_levels):
                s_n = lax.dot_general(jnp.where(head, qs[n], 0.0).astype(BF16), ks[n].astype(BF16),
                                      _NT, preferred_element_type=F32)
                sc = jnp.where(lvl == float(n + 1), s_n, sc)
            vh = load(v_ref, (2 * p + hh) * GLA_VAL_DIM, GLA_VAL_DIM)
            o_heads.append(jnp.dot(sc.astype(BF16), vh, preferred_element_type=F32))
        o = jnp.concatenate(o_heads, axis=1) + o_inter
        v_pair = load(v_ref, 2 * p * GLA_VAL_DIM, 2 * GLA_VAL_DIM)
        ds_t = lax.dot_general(v_pair, k_hat, _TN, preferred_element_type=F32)
        st_new = st * jnp.exp(e_tot) + jnp.where(own_state, ds_t, 0.0)
        s_sc[p] = st_new
        st_ref[0, p] = st_new
        for hh in range(2):
            h = 2 * p + hh
            oh = _rms(o[:n_tok, hh * GLA_VAL_DIM:(hh + 1) * GLA_VAL_DIM], gain)
            r = r_ref[0, :, h * GLA_VAL_DIM:(h + 1) * GLA_VAL_DIM]
            o_ref[0, :, h * GLA_VAL_DIM:(h + 1) * GLA_VAL_DIM] = (oh * _silu(r)).astype(BF16)


def _gla(qg, kg, vg, rg, la, s0_t, g_out):
    b, t, _ = qg.shape
    tb = GLA_TB
    tok = min(t, tb)
    nblk = max(t // tb, 1)
    m_stack, lvl, n_levels = _gla_consts(tb, GLA_CHUNK if t % GLA_CHUNK == 0 else t)
    blk = lambda w: pl.BlockSpec((1, tok, w), lambda bb, j: (bb, j, 0))
    full = lambda a: pl.BlockSpec(a.shape, lambda bb, j: (0,) * a.ndim)
    st_spec = pl.BlockSpec((1,) + s0_t.shape[1:], lambda bb, j: (bb, 0, 0, 0))
    return pl.pallas_call(
        functools.partial(_gla_body, n_levels=n_levels),
        grid=(b, nblk),
        in_specs=[blk(GLA_QK_WIDTH), blk(GLA_QK_WIDTH), blk(GLA_V_WIDTH), blk(GLA_V_WIDTH),
                  blk(GLA_QK_WIDTH), st_spec, full(g_out), full(m_stack), full(lvl)],
        out_specs=[blk(GLA_V_WIDTH), st_spec],
        out_shape=[jax.ShapeDtypeStruct((b, t, GLA_V_WIDTH), BF16),
                   jax.ShapeDtypeStruct(s0_t.shape, F32)],
        scratch_shapes=[pltpu.VMEM(s0_t.shape[1:], F32)],
        compiler_params=pltpu.CompilerParams(
            dimension_semantics=("parallel", "arbitrary"), vmem_limit_bytes=VMEM_LIMIT),
        name="gla",
    )(qg, kg, vg, rg, la, s0_t, g_out, m_stack, lvl)


def _state_to_t(s):
    b = s.shape[0]
    st = jnp.swapaxes(s, 2, 3).reshape(b, GLA_HEADS // 2, 2, GLA_VAL_DIM, GLA_KEY_DIM)
    z = jnp.zeros_like(st[:, :, 0])
    top = jnp.concatenate([st[:, :, 0], z], axis=-1)
    bot = jnp.concatenate([z, st[:, :, 1]], axis=-1)
    return jnp.concatenate([top, bot], axis=2)


def _state_from_t(st):
    b = st.shape[0]
    h0 = st[:, :, :GLA_VAL_DIM, :GLA_KEY_DIM]
    h1 = st[:, :, GLA_VAL_DIM:, GLA_KEY_DIM:]
    s = jnp.stack([h0, h1], axis=2).reshape(b, GLA_HEADS, GLA_VAL_DIM, GLA_KEY_DIM)
    return jnp.swapaxes(s, 2, 3)


def _post_body(x_ref, osb_ref, og_ref, ga1_ref, sh2_ref, sc2_ref, ga2_ref, gffn_ref, gfin_ref,
               wo_ref, wup_ref, wdn_ref, y_ref):
    mix = (jnp.dot(osb_ref[...], wo_ref[:SB_WIDTH, :], preferred_element_type=F32)
           + jnp.dot(og_ref[...], wo_ref[SB_WIDTH:, :], preferred_element_type=F32))
    x1 = x_ref[...] + ga1_ref[0] * mix
    h2 = (_rms(x1, gffn_ref[...]) * (1.0 + sc2_ref[0]) + sh2_ref[0]).astype(BF16)
    ff = jnp.zeros_like(x1)
    for c in range(D_FF // FF_CHUNK):
        u = jnp.dot(h2, wup_ref[:, c * FF_CHUNK:(c + 1) * FF_CHUNK], preferred_element_type=F32)
        a = jnp.square(jnp.maximum(u, 0.0)).astype(BF16)
        ff = ff + jnp.dot(a, wdn_ref[c * FF_CHUNK:(c + 1) * FF_CHUNK, :], preferred_element_type=F32)
    x2 = x1 + ga2_ref[0] * ff
    y_ref[...] = _rms(x2, gfin_ref[...])


def _post(x2d, o_sb, o_g, ga1, sh2, sc2, ga2, g_ffn, g_final, w_out, w_up, w_down, tm, rows_per_group):
    m, d = x2d.shape
    row = lambda w: pl.BlockSpec((tm, w), lambda i: (i, 0))
    full = lambda a: pl.BlockSpec(a.shape, lambda i: (0,) * a.ndim)
    resident = lambda a: pl.BlockSpec(a.shape, lambda i: (0,) * a.ndim, pipeline_mode=pl.Buffered(1))
    mod = lambda a: _mod_spec(a, tm, rows_per_group)
    return pl.pallas_call(
        _post_body,
        grid=(m // tm,),
        in_specs=[row(d), row(SB_WIDTH), row(GLA_V_WIDTH), mod(ga1), mod(sh2), mod(sc2), mod(ga2),
                  full(g_ffn), full(g_final), resident(w_out), resident(w_up), resident(w_down)],
        out_specs=row(d),
        out_shape=jax.ShapeDtypeStruct((m, d), F32),
        compiler_params=pltpu.CompilerParams(dimension_semantics=("parallel",),
                                             vmem_limit_bytes=VMEM_LIMIT),
        name="post_ffn",
    )(x2d, o_sb, o_g, ga1, sh2, sc2, ga2, g_ffn, g_final, w_out, w_up, w_down)


def _prep_weights(w_in, w_gla_gate, b_gla_gate, w_out, w_up, w_down):
    n_sb = 3 * SB_WIDTH
    n_gl = 2 * GLA_QK_WIDTH + 2 * GLA_V_WIDTH
    w_sb = w_in[:, :n_sb].astype(BF16)
    a_pad = jnp.pad(w_in[:, n_sb + n_gl:], ((0, 0), (0, LANES - GLA_GATE_RANK)))
    w_gl = jnp.concatenate([w_in[:, n_sb:n_sb + n_gl], a_pad], axis=1).astype(BF16)
    w_gate = jnp.pad(w_gla_gate, ((0, LANES - GLA_GATE_RANK), (0, 0))).astype(BF16)
    return (w_sb, w_gl, w_gate, b_gla_gate.reshape(1, -1),
            w_out.astype(BF16), w_up.astype(BF16), w_down.astype(BF16))


def _layer(x, mods, weights, norms, sb_fn, s0_t, tm):
    g_rows, t, d = x.shape
    sh1, sc1, ga1, sh2, sc2, ga2 = mods
    w_sb, w_gl, w_gate, b_gate, w_out, w_up, w_down = weights
    g_mix, g_gla_out, g_ffn, g_final = norms
    x2d = x.reshape(g_rows * t, d)
    rows_per_group = t if sh1.shape[1] == 1 else g_rows * t
    q, k, v, qg, kg, vg, rg, la = _inproj(x2d, sh1, sc1, g_mix, w_sb, w_gl, w_gate, b_gate,
                                          tm, rows_per_group)
    r3 = lambda a: a.reshape(g_rows, t, a.shape[-1])
    o_sb = sb_fn(r3(q), r3(k), r3(v))
    o_g, st = _gla(r3(qg), r3(kg), r3(vg), r3(rg), r3(la), s0_t, g_gla_out)
    y = _post(x2d, o_sb.reshape(g_rows * t, -1), o_g.reshape(g_rows * t, -1), ga1, sh2, sc2, ga2,
              g_ffn, g_final, w_out, w_up, w_down, tm, rows_per_group)
    return (y.reshape(g_rows, t, d), k.reshape(g_rows, t, SB_HEADS, SB_HEAD_DIM),
            v.reshape(g_rows, t, SB_HEADS, SB_HEAD_DIM), _state_from_t(st))


def kernel(x_prompt, x_sample, c_prompt, c_sample, cache_k, cache_v, state_gla, page_table, w_ada, b_ada, g_mix, w_in, b_sb, w_gla_gate, b_gla_gate, g_gla_out, w_out, g_ffn, w_up, w_down, g_final):
    depth = w_ada.shape[0]
    assert depth == 1, "final-norm fusion assumes a single layer"
    nb, t, d = x_prompt.shape
    nd, td, _ = x_sample.shape
    l = 0

    c_all = jnp.concatenate([c_prompt, c_sample], axis=0)
    pad_rows = (-c_all.shape[0]) % SUBLANES
    mod = _ada(jnp.pad(c_all, ((0, pad_rows), (0, 0))), w_ada[l], b_ada[l])
    mods_p = [m[:nb].reshape(nb, 1, d) for m in jnp.split(mod, 6, axis=-1)]
    mods_s = [jnp.repeat(m[nb:nb + nd], td, axis=0).reshape(1, nd * td, d)
              for m in jnp.split(mod, 6, axis=-1)]

    weights = _prep_weights(w_in[l], w_gla_gate[l], b_gla_gate[l], w_out[l], w_up[l], w_down[l])
    norms = (g_mix[l].reshape(1, d), g_gla_out[l].reshape(1, -1), g_ffn[l].reshape(1, d),
             g_final.reshape(1, d))

    sb_p = lambda q, k, v: _sb_prompt(q, k, v, b_sb[l])
    zeros_t = jnp.zeros((nb, GLA_HEADS // 2, 2 * GLA_VAL_DIM, 2 * GLA_KEY_DIM), F32)
    y_p, k_p, v_p, s_p = _layer(x_prompt, mods_p, weights, norms, sb_p, zeros_t, min(ROW_TILE, t))

    sb_s = lambda q, k, v: _sb_decode(q, k, v, cache_k[l], cache_v[l], page_table, b_sb[l])
    y_s, k_s, v_s, s_s = _layer(x_sample, mods_s, weights, norms, sb_s, _state_to_t(state_gla[l]),
                                nd * td)

    return (y_p, y_s, k_p[None], v_p[None], s_p[None], k_s[None], v_s[None], s_s[None])
```

```python
import functools

import numpy as np
import jax
import jax.numpy as jnp
from jax import lax
from jax.experimental import pallas as pl
from jax.experimental.pallas import tpu as pltpu

F32 = jnp.float32
BF16 = jnp.bfloat16

D_MODEL = 1024
SB_HEADS = 8
SB_HEAD_DIM = 64
SB_WIDTH = SB_HEADS * SB_HEAD_DIM
GLA_HEADS = 4
GLA_KEY_DIM = 64
GLA_VAL_DIM = 128
GLA_QK_WIDTH = GLA_HEADS * GLA_KEY_DIM
GLA_V_WIDTH = GLA_HEADS * GLA_VAL_DIM
GLA_GATE_RANK = 16
GLA_GATE_TEMP = 16.0
GLA_CHUNK = 16
D_FF = 4 * D_MODEL
NORM_EPS = 1e-6

LANES = 128
SUBLANES = 8
VMEM_LIMIT = 56 * 1024 * 1024

SB_BQ = 512
SB_BK = 256
GLA_TB = 128
DEC_PAGES_PER_STEP = 16
ROW_TILE = 512
ADA_TN = 1024
FF_CHUNK = 1024
SB_MASKED = -1e30

_NT = (((1,), (1,)), ((), ()))
_TN = (((0,), (0,)), ((), ()))


LOG2E = float(np.log2(np.e))


def _softplus2(z2):
    return jnp.maximum(z2, 0.0) + jnp.log(1.0 + jnp.exp2(-jnp.abs(z2))) * LOG2E


def _silu(x):
    return x * jax.nn.sigmoid(x)


def _rms(x, gain):
    ms = jnp.mean(x * x, axis=-1, keepdims=True)
    return x * lax.rsqrt(ms + NORM_EPS) * gain


def _ada_body(c_ref, w_ref, b_ref, o_ref):
    s = _silu(c_ref[...])
    o_ref[...] = jnp.dot(s.astype(BF16), w_ref[...].astype(BF16),
                         preferred_element_type=F32) + b_ref[...]


def _ada(c, w_ada, b_ada):
    rows, d = c.shape
    n = w_ada.shape[1]
    return pl.pallas_call(
        _ada_body,
        grid=(n // ADA_TN,),
        in_specs=[pl.BlockSpec((rows, d), lambda j: (0, 0)),
                  pl.BlockSpec((d, ADA_TN), lambda j: (0, j)),
                  pl.BlockSpec((1, ADA_TN), lambda j: (0, j))],
        out_specs=pl.BlockSpec((rows, ADA_TN), lambda j: (0, j)),
        out_shape=jax.ShapeDtypeStruct((rows, n), F32),
        name="ada_mod",
    )(c, w_ada, b_ada.reshape(1, n))


def _inproj_body(x_ref, sh_ref, sc_ref, g_ref, wsb_ref, wgl_ref, wgate_ref, bgate_ref,
                 q_ref, k_ref, v_ref, qg_ref, kg_ref, vg_ref, rg_ref, la_ref):
    h = _rms(x_ref[...], g_ref[...]) * (1.0 + sc_ref[0]) + sh_ref[0]
    hb = h.astype(BF16)
    p1 = jnp.dot(hb, wsb_ref[...], preferred_element_type=F32)
    q_ref[...] = (p1[:, :SB_WIDTH] * (LOG2E * SB_HEAD_DIM ** -0.5)).astype(BF16)
    k_ref[...] = p1[:, SB_WIDTH:2 * SB_WIDTH]
    v_ref[...] = p1[:, 2 * SB_WIDTH:3 * SB_WIDTH]
    p2 = jnp.dot(hb, wgl_ref[...], preferred_element_type=F32)
    o = 0
    qg_ref[...] = p2[:, o:o + GLA_QK_WIDTH] * (GLA_KEY_DIM ** -0.5)
    o += GLA_QK_WIDTH
    kg_ref[...] = p2[:, o:o + GLA_QK_WIDTH]
    o += GLA_QK_WIDTH
    vg_ref[...] = p2[:, o:o + GLA_V_WIDTH].astype(BF16)
    o += GLA_V_WIDTH
    rg_ref[...] = p2[:, o:o + GLA_V_WIDTH]
    o += GLA_V_WIDTH
    a_low = p2[:, o:o + LANES]
    xg = jnp.dot(a_low.astype(BF16), wgate_ref[...], preferred_element_type=F32) + bgate_ref[...]
    la_ref[...] = (jnp.minimum(xg, 0.0) - jnp.log(1.0 + jnp.exp(-jnp.abs(xg)))) * (1.0 / GLA_GATE_TEMP)


def _mod_spec(mod, tm, rows_per_group):
    if mod.shape[1] == 1:
        tiles_per_group = rows_per_group // tm
        return pl.BlockSpec((1, 1, mod.shape[2]), lambda i: (i // tiles_per_group, 0, 0))
    return pl.BlockSpec((1, tm, mod.shape[2]), lambda i: (0, i, 0))


def _inproj(x2d, sh, sc, g_mix, w_sb, w_gl, w_gate, b_gate, tm, rows_per_group):
    m, d = x2d.shape
    row = lambda w: pl.BlockSpec((tm, w), lambda i: (i, 0))
    full = lambda a: pl.BlockSpec(a.shape, lambda i: (0,) * a.ndim)
    outs = [(SB_WIDTH, BF16), (SB_WIDTH, F32), (SB_WIDTH, F32), (GLA_QK_WIDTH, F32),
            (GLA_QK_WIDTH, F32), (GLA_V_WIDTH, BF16), (GLA_V_WIDTH, F32), (GLA_QK_WIDTH, F32)]
    return pl.pallas_call(
        _inproj_body,
        grid=(m // tm,),
        in_specs=[row(d), _mod_spec(sh, tm, rows_per_group), _mod_spec(sc, tm, rows_per_group),
                  full(g_mix), full(w_sb), full(w_gl), full(w_gate), full(b_gate)],
        out_specs=[row(w) for w, _ in outs],
        out_shape=[jax.ShapeDtypeStruct((m, w), dt) for w, dt in outs],
        compiler_params=pltpu.CompilerParams(dimension_semantics=("parallel",),
                                             vmem_limit_bytes=VMEM_LIMIT),
        name="in_proj",
    )(x2d, sh, sc, g_mix, w_sb, w_gl, w_gate, b_gate)


def _sb_prompt_body(bias_ref, q_ref, k_ref, v_ref, u_ref, o_ref,
                    mask_sc, z_sc, e_sc, cf_sc, carry_sc, acc_sc):
    bq = q_ref.shape[0]
    bk = u_ref.shape[0]
    r = bq // bk
    p = pl.program_id(1)
    i = pl.program_id(2)
    n = r * (i + 1)

    @pl.when(i == 0)
    def _():
        row = lax.broadcasted_iota(jnp.int32, (bq, bk), 0)
        col = lax.broadcasted_iota(jnp.int32, (bq, bk), 1)
        for hh in range(2):
            b = bias_ref[2 * p + hh] * LOG2E
            for t in range(r):
                mask_sc[hh, t] = jnp.where((r - 1 - t) * bk + col < row, b, SB_MASKED)
            mask_sc[hh, r] = jnp.full((bq, bk), b, F32)
            mask_sc[hh, r + 1] = jnp.full((bq, bk), SB_MASKED, F32)

    q = q_ref[...]
    lane = lax.broadcasted_iota(jnp.int32, (bq, LANES), 1)
    first_head = lane < SB_HEAD_DIM
    qh = (jnp.where(first_head, q, jnp.zeros_like(q)), jnp.where(first_head, jnp.zeros_like(q), q))
    dead = jnp.full((bq, bk), SB_MASKED, F32)
    for hh in range(2):
        z_sc[1, hh] = dead
        e_sc[0, hh] = dead
        e_sc[1, hh] = dead
        cf_sc[0, hh] = jnp.zeros((bq, LANES), F32)
        cf_sc[1, hh] = jnp.zeros((bq, LANES), F32)
        carry_sc[hh] = jnp.zeros((bq, LANES), F32)
        acc_sc[hh] = jnp.zeros((bq, LANES), F32)

    def tile_start(t):
        j = r * i + r - 1 - jnp.clip(t, 0, n - 1)
        return pl.multiple_of(j * bk, bk)

    def iteration(t, slot):
        vb = v_ref[pl.ds(tile_start(t - 2), bk), :].astype(BF16)
        for hh in range(2):
            w = jnp.exp2(e_sc[slot, hh])
            acc_sc[hh] += cf_sc[slot, hh] * jnp.dot(w.astype(BF16), vb, preferred_element_type=F32)
        for hh in range(2):
            z = z_sc[1 - slot, hh]
            c = carry_sc[hh]
            a = jnp.dot(_softplus2(z).astype(BF16), u_ref[...], preferred_element_type=F32)
            e_sc[1 - slot, hh] = z - a
            cf_sc[1 - slot, hh] = jnp.exp2(-c)
            carry_sc[hh] = c + jnp.broadcast_to(a[:, 0:1], c.shape)
        kb = k_ref[pl.ds(tile_start(t), bk), :].astype(BF16)
        m = jnp.where(t < n, jnp.minimum(t, r), r + 1)
        for hh in range(2):
            z_sc[slot, hh] = (lax.dot_general(qh[hh], kb, _NT, preferred_element_type=F32)
                              + mask_sc[hh, m])

    def body(tt, _):
        iteration(2 * tt, 0)
        iteration(2 * tt + 1, 1)
        return 0

    lax.fori_loop(0, (n + 3) // 2, body, 0)
    o_ref[...] = jnp.where(first_head, acc_sc[0], acc_sc[1]).astype(BF16)


def _suffix_ones(n):
    r = np.arange(n)
    return jnp.asarray((r[:, None] >= r[None, :]).astype(np.float32), dtype=BF16)


def _sb_prompt(q, k, v, b_sb):
    b, t, w = q.shape
    bq = min(SB_BQ, t)
    bk = min(SB_BK, t)
    pairs = w // LANES
    return pl.pallas_call(
        _sb_prompt_body,
        grid=(b, pairs, t // bq),
        in_specs=[pl.BlockSpec(memory_space=pltpu.SMEM),
                  pl.BlockSpec((None, bq, LANES), lambda bb, p, i: (bb, i, p)),
                  pl.BlockSpec((None, t, LANES), lambda bb, p, i: (bb, 0, p)),
                  pl.BlockSpec((None, t, LANES), lambda bb, p, i: (bb, 0, p)),
                  pl.BlockSpec((bk, bk), lambda bb, p, i: (0, 0))],
        out_specs=pl.BlockSpec((None, bq, LANES), lambda bb, p, i: (bb, i, p)),
        out_shape=jax.ShapeDtypeStruct((b, t, w), BF16),
        scratch_shapes=[pltpu.VMEM((2, bq // bk + 2, bq, bk), F32),
                        pltpu.VMEM((2, 2, bq, bk), F32),
                        pltpu.VMEM((2, 2, bq, bk), F32),
                        pltpu.VMEM((2, 2, bq, LANES), F32),
                        pltpu.VMEM((2, bq, LANES), F32),
                        pltpu.VMEM((2, bq, LANES), F32)],
        compiler_params=pltpu.CompilerParams(
            dimension_semantics=("parallel", "parallel", "arbitrary"),
            vmem_limit_bytes=VMEM_LIMIT),
        name="sb_prompt",
    )(b_sb, q, k, v, _suffix_ones(bk))


def _sb_new_keys_tile(qh, kh, vh, u, bias, causal):
    n_tok = qh[0].shape[0]
    z = jnp.concatenate([lax.dot_general(qh[h], kh[h], _NT, preferred_element_type=F32)
                         for h in range(SB_HEADS)], axis=0) + bias
    sp = jnp.where(causal, _softplus2(z), 0.0)
    a = jnp.dot(sp.astype(BF16), u, preferred_element_type=F32)
    w = jnp.where(causal, jnp.exp2(z - a), 0.0)
    pv = jnp.concatenate([jnp.dot(w[h * n_tok:(h + 1) * n_tok].astype(BF16), vh[h],
                                  preferred_element_type=F32) for h in range(SB_HEADS)], axis=0)
    return jnp.broadcast_to(a[:, 0:1], (a.shape[0], LANES)), pv


def _sb_decode_body(pt_ref, bias_ref, q_ref, kn_ref, vn_ref, *rest):
    del pt_ref
    npg = DEC_PAGES_PER_STEP
    k_pages = rest[:npg]
    v_pages = rest[npg:2 * npg]
    u_ref, o_ref, carry_sc, acc_sc = rest[2 * npg:]
    g = pl.program_id(1)
    n_tok = q_ref.shape[1]
    rows = SB_HEADS * n_tok
    page = k_pages[0].shape[2]
    heads = range(SB_HEADS)
    head_lanes = lambda x, h: x[:, h * SB_HEAD_DIM:(h + 1) * SB_HEAD_DIM]

    q = q_ref[0].astype(F32)
    qh = [head_lanes(q, h).astype(BF16) for h in heads]
    row_p = lax.broadcasted_iota(jnp.int32, (rows, page), 0)
    col_p = lax.broadcasted_iota(jnp.int32, (rows, page), 1)
    bias = jnp.zeros((rows, page), F32)
    for h in heads:
        bias = jnp.where(row_p // n_tok == h, bias_ref[h] * LOG2E, bias)
    u = u_ref[...]

    @pl.when(g == 0)
    def _():
        pad = jnp.zeros((page - n_tok, SB_HEAD_DIM), F32)
        kh = [jnp.concatenate([head_lanes(kn_ref[0], h), pad], axis=0).astype(BF16) for h in heads]
        vh = [jnp.concatenate([head_lanes(vn_ref[0], h), pad], axis=0).astype(BF16) for h in heads]
        causal = col_p < (row_p % n_tok)
        carry_sc[...], acc_sc[...] = _sb_new_keys_tile(qh, kh, vh, u, bias, causal)

    order = list(reversed(range(npg)))
    zs = [jnp.concatenate([jnp.dot(qh[h], k_pages[n][h].astype(BF16), preferred_element_type=F32)
                           for h in heads], axis=0) + bias for n in order]
    sp = _softplus2(jnp.concatenate(zs, axis=0))
    a_all = jnp.dot(sp.astype(BF16), u, preferred_element_type=F32)
    c = carry_sc[...]
    a = acc_sc[...]
    for i, n in enumerate(order):
        a_i = a_all[i * rows:(i + 1) * rows]
        w = jnp.exp2(zs[i] - a_i - c)
        c = c + jnp.broadcast_to(a_i[:, 0:1], c.shape)
        a = a + jnp.concatenate(
            [lax.dot_general(w[h * n_tok:(h + 1) * n_tok].astype(BF16), v_pages[n][h].astype(BF16),
                             _NT, preferred_element_type=F32) for h in heads], axis=0)
    carry_sc[...] = c
    acc_sc[...] = a

    @pl.when(g == pl.num_programs(1) - 1)
    def _():
        o_ref[0] = a.astype(BF16)


def _sb_decode(q, k_new, v_new, cache_k, cache_v, page_table, b_sb):
    nb, n_tok, w = q.shape
    n_pages = page_table.shape[1]
    page = cache_k.shape[1]
    npg = DEC_PAGES_PER_STEP
    groups = n_pages // npg
    rows = SB_HEADS * n_tok
    cache_k = jnp.transpose(cache_k, (0, 2, 3, 1))
    cache_v = jnp.transpose(cache_v, (0, 2, 3, 1))

    def page_map(n):
        return lambda b, g, pt: (pt[b, (groups - 1 - g) * npg + n], 0, 0, 0)

    tok_spec = pl.BlockSpec((1, n_tok, w), lambda b, g, pt: (b, 0, 0))
    page_specs = [pl.BlockSpec((None, SB_HEADS, SB_HEAD_DIM, page), page_map(n)) for n in range(npg)]
    out_spec = pl.BlockSpec((1, rows, SB_HEAD_DIM), lambda b, g, pt: (b, 0, 0))
    grid_spec = pltpu.PrefetchScalarGridSpec(
        num_scalar_prefetch=1,
        grid=(nb, groups),
        in_specs=[pl.BlockSpec(memory_space=pltpu.SMEM), tok_spec, tok_spec, tok_spec]
        + page_specs + page_specs
        + [pl.BlockSpec((page, page), lambda b, g, pt: (0, 0))],
        out_specs=out_spec,
        scratch_shapes=[pltpu.VMEM((rows, LANES), F32), pltpu.VMEM((rows, SB_HEAD_DIM), F32)],
    )
    o = pl.pallas_call(
        _sb_decode_body,
        grid_spec=grid_spec,
        out_shape=jax.ShapeDtypeStruct((nb, rows, SB_HEAD_DIM), BF16),
        compiler_params=pltpu.CompilerParams(
            dimension_semantics=("parallel", "arbitrary"), vmem_limit_bytes=VMEM_LIMIT),
        name="sb_decode",
    )(page_table, b_sb, q, k_new, v_new, *([cache_k] * npg), *([cache_v] * npg), _suffix_ones(page))
    return o.reshape(nb, SB_HEADS, n_tok, SB_HEAD_DIM).transpose(0, 2, 1, 3).reshape(nb, n_tok, w)


def _gla_consts(tb, chunk):
    i = np.arange(tb)[:, None]
    t = np.arange(tb)[None, :]
    mats = []
    lvl = np.zeros((tb, tb), np.float32)
    code = 1
    s = tb
    while s > chunk:
        mid = (i // s) * s + s // 2
        upper = i >= mid
        mats.append(np.where(upper & (t >= mid) & (t <= i), 1.0, 0.0)
                    + np.where(~upper & (t > i) & (t < mid), -1.0, 0.0))
        mid_t = (t // s) * s + s // 2
        lvl = np.where((i // s == t // s) & upper & (t < mid_t), code, lvl)
        code += 1
        s //= 2
    mats.append(np.where((t // chunk == i // chunk) & (t <= i), 1.0, 0.0))
    lvl = np.where((i // chunk == t // chunk) & (t <= i), code, lvl)
    mats.append(np.where(t <= i, 1.0, 0.0))
    stack = np.concatenate(mats, axis=0).astype(np.float32)
    return jnp.asarray(stack, dtype=BF16), jnp.asarray(lvl, dtype=F32), code


def _gla_body(q_ref, k_ref, v_ref, r_ref, la_ref, s0_ref, g_ref, m_ref, lvl_ref,
              o_ref, st_ref, s_sc, *, n_levels):
    tb = lvl_ref.shape[0]
    n_tok = q_ref.shape[1]
    blk = pl.program_id(1)

    @pl.when(blk == 0)
    def _():
        s_sc[...] = s0_ref[0]

    def load(ref, lo, width):
        x = ref[0, :, lo:lo + width]
        if n_tok < tb:
            x = jnp.concatenate([x.astype(F32), jnp.zeros((tb - n_tok, width), F32)], axis=0).astype(x.dtype)
        return x

    lvl = lvl_ref[...]
    lane = lax.broadcasted_iota(jnp.int32, (tb, LANES), 1)
    first_head = lane < GLA_KEY_DIM
    row_s = lax.broadcasted_iota(jnp.int32, (2 * GLA_VAL_DIM, LANES), 0)
    lane_s = lax.broadcasted_iota(jnp.int32, (2 * GLA_VAL_DIM, LANES), 1)
    own_state = (row_s < GLA_VAL_DIM) == (lane_s < GLA_KEY_DIM)
    gain = g_ref[...]

    for p in range(GLA_HEADS // 2):
        la = load(la_ref, p * LANES, LANES)
        hi = la.astype(BF16)
        lo = (la - hi.astype(F32)).astype(BF16)
        res = jnp.dot(m_ref[...], jnp.concatenate([hi, lo], axis=1), preferred_element_type=F32)
        dall = res[:, :LANES] + res[:, LANES:]
        d = [dall[n * tb:(n + 1) * tb] for n in range(n_levels + 1)]
        bc = d[n_levels]
        e_tot = bc[tb - 1:tb, :]
        q = load(q_ref, p * LANES, LANES)
        k = load(k_ref, p * LANES, LANES)
        qs, ks = [], []
        for n in range(n_levels - 1):
            x = jnp.exp(-jnp.abs(d[n]))
            qs.append(q * x)
            ks.append(k * x)
        qs.append(q * jnp.exp(d[n_levels - 1]))
        ks.append(k * jnp.exp(-d[n_levels - 1]))
        q_hat = (q * jnp.exp(bc)).astype(BF16)
        k_hat = (k * jnp.exp(e_tot - bc)).astype(BF16)
        st = s_sc[p]
        o_inter = lax.dot_general(q_hat, st.astype(BF16), _NT, preferred_element_type=F32)
        o_heads = []
        for hh in range(2):
            head = first_head if hh == 0 else jnp.logical_not(first_head)
            sc = jnp.zeros((tb, tb), F32)
            for n in range(n_levels):
                s_n = lax.dot_general(jnp.where(head, qs[n], 0.0).astype(BF16), ks[n].astype(BF16),
                                      _NT, preferred_element_type=F32)
                sc = jnp.where(lvl == float(n + 1), s_n, sc)
            vh = load(v_ref, (2 * p + hh) * GLA_VAL_DIM, GLA_VAL_DIM)
            o_heads.append(jnp.dot(sc.astype(BF16), vh, preferred_element_type=F32))
        o = jnp.concatenate(o_heads, axis=1) + o_inter
        v_pair = load(v_ref, 2 * p * GLA_VAL_DIM, 2 * GLA_VAL_DIM)
        ds_t = lax.dot_general(v_pair, k_hat, _TN, preferred_element_type=F32)
        st_new = st * jnp.exp(e_tot) + jnp.where(own_state, ds_t, 0.0)
        s_sc[p] = st_new
        st_ref[0, p] = st_new
        for hh in range(2):
            h = 2 * p + hh
            oh = _rms(o[:n_tok, hh * GLA_VAL_DIM:(hh + 1) * GLA_VAL_DIM], gain)
            r = r_ref[0, :, h * GLA_VAL_DIM:(h + 1) * GLA_VAL_DIM]
            o_ref[0, :, h * GLA_VAL_DIM:(h + 1) * GLA_VAL_DIM] = (oh * _silu(r)).astype(BF16)


def _gla(qg, kg, vg, rg, la, s0_t, g_out):
    b, t, _ = qg.shape
    tb = GLA_TB
    tok = min(t, tb)
    nblk = max(t // tb, 1)
    m_stack, lvl, n_levels = _gla_consts(tb, GLA_CHUNK if t % GLA_CHUNK == 0 else t)
    blk = lambda w: pl.BlockSpec((1, tok, w), lambda bb, j: (bb, j, 0))
    full = lambda a: pl.BlockSpec(a.shape, lambda bb, j: (0,) * a.ndim)
    st_spec = pl.BlockSpec((1,) + s0_t.shape[1:], lambda bb, j: (bb, 0, 0, 0))
    return pl.pallas_call(
        functools.partial(_gla_body, n_levels=n_levels),
        grid=(b, nblk),
        in_specs=[blk(GLA_QK_WIDTH), blk(GLA_QK_WIDTH), blk(GLA_V_WIDTH), blk(GLA_V_WIDTH),
                  blk(GLA_QK_WIDTH), st_spec, full(g_out), full(m_stack), full(lvl)],
        out_specs=[blk(GLA_V_WIDTH), st_spec],
        out_shape=[jax.ShapeDtypeStruct((b, t, GLA_V_WIDTH), BF16),
                   jax.ShapeDtypeStruct(s0_t.shape, F32)],
        scratch_shapes=[pltpu.VMEM(s0_t.shape[1:], F32)],
        compiler_params=pltpu.CompilerParams(
            dimension_semantics=("parallel", "arbitrary"), vmem_limit_bytes=VMEM_LIMIT),
        name="gla",
    )(qg, kg, vg, rg, la, s0_t, g_out, m_stack, lvl)


def _state_to_t(s):
    b = s.shape[0]
    st = jnp.swapaxes(s, 2, 3).reshape(b, GLA_HEADS // 2, 2, GLA_VAL_DIM, GLA_KEY_DIM)
    z = jnp.zeros_like(st[:, :, 0])
    top = jnp.concatenate([st[:, :, 0], z], axis=-1)
    bot = jnp.concatenate([z, st[:, :, 1]], axis=-1)
    return jnp.concatenate([top, bot], axis=2)


def _state_from_t(st):
    b = st.shape[0]
    h0 = st[:, :, :GLA_VAL_DIM, :GLA_KEY_DIM]
    h1 = st[:, :, GLA_VAL_DIM:, GLA_KEY_DIM:]
    s = jnp.stack([h0, h1], axis=2).reshape(b, GLA_HEADS, GLA_VAL_DIM, GLA_KEY_DIM)
    return jnp.swapaxes(s, 2, 3)


def _post_body(x_ref, osb_ref, og_ref, ga1_ref, sh2_ref, sc2_ref, ga2_ref, gffn_ref, gfin_ref,
               wo_ref, wup_ref, wdn_ref, y_ref):
    mix = (jnp.dot(osb_ref[...], wo_ref[:SB_WIDTH, :], preferred_element_type=F32)
           + jnp.dot(og_ref[...], wo_ref[SB_WIDTH:, :], preferred_element_type=F32))
    x1 = x_ref[...] + ga1_ref[0] * mix
    h2 = (_rms(x1, gffn_ref[...]) * (1.0 + sc2_ref[0]) + sh2_ref[0]).astype(BF16)
    ff = jnp.zeros_like(x1)
    for c in range(D_FF // FF_CHUNK):
        u = jnp.dot(h2, wup_ref[:, c * FF_CHUNK:(c + 1) * FF_CHUNK], preferred_element_type=F32)
        a = jnp.square(jnp.maximum(u, 0.0)).astype(BF16)
        ff = ff + jnp.dot(a, wdn_ref[c * FF_CHUNK:(c + 1) * FF_CHUNK, :], preferred_element_type=F32)
    x2 = x1 + ga2_ref[0] * ff
    y_ref[...] = _rms(x2, gfin_ref[...])


def _post(x2d, o_sb, o_g, ga1, sh2, sc2, ga2, g_ffn, g_final, w_out, w_up, w_down, tm, rows_per_group):
    m, d = x2d.shape
    row = lambda w: pl.BlockSpec((tm, w), lambda i: (i, 0))
    full = lambda a: pl.BlockSpec(a.shape, lambda i: (0,) * a.ndim)
    resident = lambda a: pl.BlockSpec(a.shape, lambda i: (0,) * a.ndim, pipeline_mode=pl.Buffered(1))
    mod = lambda a: _mod_spec(a, tm, rows_per_group)
    return pl.pallas_call(
        _post_body,
        grid=(m // tm,),
        in_specs=[row(d), row(SB_WIDTH), row(GLA_V_WIDTH), mod(ga1), mod(sh2), mod(sc2), mod(ga2),
                  full(g_ffn), full(g_final), resident(w_out), resident(w_up), resident(w_down)],
        out_specs=row(d),
        out_shape=jax.ShapeDtypeStruct((m, d), F32),
        compiler_params=pltpu.CompilerParams(dimension_semantics=("parallel",),
                                             vmem_limit_bytes=VMEM_LIMIT),
        name="post_ffn",
    )(x2d, o_sb, o_g, ga1, sh2, sc2, ga2, g_ffn, g_final, w_out, w_up, w_down)


def _prep_weights(w_in, w_gla_gate, b_gla_gate, w_out, w_up, w_down):
    n_sb = 3 * SB_WIDTH
    n_gl = 2 * GLA_QK_WIDTH + 2 * GLA_V_WIDTH
    w_sb = w_in[:, :n_sb].astype(BF16)
    a_pad = jnp.pad(w_in[:, n_sb + n_gl:], ((0, 0), (0, LANES - GLA_GATE_RANK)))
    w_gl = jnp.concatenate([w_in[:, n_sb:n_sb + n_gl], a_pad], axis=1).astype(BF16)
    w_gate = jnp.pad(w_gla_gate, ((0, LANES - GLA_GATE_RANK), (0, 0))).astype(BF16)
    return (w_sb, w_gl, w_gate, b_gla_gate.reshape(1, -1),
            w_out.astype(BF16), w_up.astype(BF16), w_down.astype(BF16))


def _layer(x, mods, weights, norms, sb_fn, s0_t, tm):
    g_rows, t, d = x.shape
    sh1, sc1, ga1, sh2, sc2, ga2 = mods
    w_sb, w_gl, w_gate, b_gate, w_out, w_up, w_down = weights
    g_mix, g_gla_out, g_ffn, g_final = norms
    x2d = x.reshape(g_rows * t, d)
    rows_per_group = t if sh1.shape[1] == 1 else g_rows * t
    q, k, v, qg, kg, vg, rg, la = _inproj(x2d, sh1, sc1, g_mix, w_sb, w_gl, w_gate, b_gate,
                                          tm, rows_per_group)
    r3 = lambda a: a.reshape(g_rows, t, a.shape[-1])
    o_sb = sb_fn(r3(q), r3(k), r3(v))
    o_g, st = _gla(r3(qg), r3(kg), r3(vg), r3(rg), r3(la), s0_t, g_gla_out)
    y = _post(x2d, o_sb.reshape(g_rows * t, -1), o_g.reshape(g_rows * t, -1), ga1, sh2, sc2, ga2,
              g_ffn, g_final, w_out, w_up, w_down, tm, rows_per_group)
    return (y.reshape(g_rows, t, d), k.reshape(g_rows, t, SB_HEADS, SB_HEAD_DIM),
            v.reshape(g_rows, t, SB_HEADS, SB_HEAD_DIM), _state_from_t(st))


def kernel(x_prompt, x_sample, c_prompt, c_sample, cache_k, cache_v, state_gla, page_table, w_ada, b_ada, g_mix, w_in, b_sb, w_gla_gate, b_gla_gate, g_gla_out, w_out, g_ffn, w_up, w_down, g_final):
    depth = w_ada.shape[0]
    assert depth == 1, "final-norm fusion assumes a single layer"
    nb, t, d = x_prompt.shape
    nd, td, _ = x_sample.shape
    l = 0

    c_all = jnp.concatenate([c_prompt, c_sample], axis=0)
    pad_rows = (-c_all.shape[0]) % SUBLANES
    mod = _ada(jnp.pad(c_all, ((0, pad_rows), (0, 0))), w_ada[l], b_ada[l])
    mods_p = [m[:nb].reshape(nb, 1, d) for m in jnp.split(mod, 6, axis=-1)]
    mods_s = [jnp.repeat(m[nb:nb + nd], td, axis=0).reshape(1, nd * td, d)
              for m in jnp.split(mod, 6, axis=-1)]

    weights = _prep_weights(w_in[l], w_gla_gate[l], b_gla_gate[l], w_out[l], w_up[l], w_down[l])
    norms = (g_mix[l].reshape(1, d), g_gla_out[l].reshape(1, -1), g_ffn[l].reshape(1, d),
             g_final.reshape(1, d))

    sb_p = lambda q, k, v: _sb_prompt(q, k, v, b_sb[l])
    zeros_t = jnp.zeros((nb, GLA_HEADS // 2, 2 * GLA_VAL_DIM, 2 * GLA_KEY_DIM), F32)
    y_p, k_p, v_p, s_p = _layer(x_prompt, mods_p, weights, norms, sb_p, zeros_t, min(ROW_TILE, t))

    sb_s = lambda q, k, v: _sb_decode(q, k, v, cache_k[l], cache_v[l], page_table, b_sb[l])
    y_s, k_s, v_s, s_s = _layer(x_sample, mods_s, weights, norms, sb_s, _state_to_t(state_gla[l]),
                                nd * td)

    return (y_p, y_s, k_p[None], v_p[None], s_p[None], k_s[None], v_s[None], s_s[None])
```

```python
import functools

import numpy as np
import jax
import jax.numpy as jnp
from jax import lax
from jax.experimental import pallas as pl
from jax.experimental.pallas import tpu as pltpu

F32 = jnp.float32
BF16 = jnp.bfloat16

D_MODEL = 1024
SB_HEADS = 8
SB_HEAD_DIM = 64
SB_WIDTH = SB_HEADS * SB_HEAD_DIM
GLA_HEADS = 4
GLA_KEY_DIM = 64
GLA_VAL_DIM = 128
GLA_QK_WIDTH = GLA_HEADS * GLA_KEY_DIM
GLA_V_WIDTH = GLA_HEADS * GLA_VAL_DIM
GLA_GATE_RANK = 16
GLA_GATE_TEMP = 16.0
GLA_CHUNK = 16
D_FF = 4 * D_MODEL
NORM_EPS = 1e-6

LANES = 128
SUBLANES = 8
VMEM_LIMIT = 56 * 1024 * 1024

SB_BQ = 512
SB_BK = 256
GLA_TB = 128
GLA_STEP_BLOCKS = 4
DEC_PAGES_PER_STEP = 16
ROW_TILE = 512
ADA_TN = 1024
FF_CHUNK = 1024
SB_MASKED = -1e30

_NT = (((1,), (1,)), ((), ()))
_TN = (((0,), (0,)), ((), ()))


LOG2E = float(np.log2(np.e))


def _softplus2(z2):
    return jnp.maximum(z2, 0.0) + jnp.log(1.0 + jnp.exp2(-jnp.abs(z2))) * LOG2E


def _silu(x):
    return x * jax.nn.sigmoid(x)


def _rms(x, gain):
    ms = jnp.mean(x * x, axis=-1, keepdims=True)
    return x * lax.rsqrt(ms + NORM_EPS) * gain


def _ada_body(c_ref, w_ref, b_ref, o_ref):
    s = _silu(c_ref[...])
    o_ref[...] = jnp.dot(s.astype(BF16), w_ref[...].astype(BF16),
                         preferred_element_type=F32) + b_ref[...]


def _ada(c, w_ada, b_ada):
    rows, d = c.shape
    n = w_ada.shape[1]
    return pl.pallas_call(
        _ada_body,
        grid=(n // ADA_TN,),
        in_specs=[pl.BlockSpec((rows, d), lambda j: (0, 0)),
                  pl.BlockSpec((d, ADA_TN), lambda j: (0, j)),
                  pl.BlockSpec((1, ADA_TN), lambda j: (0, j))],
        out_specs=pl.BlockSpec((rows, ADA_TN), lambda j: (0, j)),
        out_shape=jax.ShapeDtypeStruct((rows, n), F32),
        name="ada_mod",
    )(c, w_ada, b_ada.reshape(1, n))


def _inproj_body(x_ref, sh_ref, sc_ref, g_ref, wsb_ref, wgl_ref, wgate_ref, bgate_ref,
                 q_ref, k_ref, v_ref, qg_ref, kg_ref, vg_ref, rg_ref, la_ref):
    h = _rms(x_ref[...], g_ref[...]) * (1.0 + sc_ref[0]) + sh_ref[0]
    hb = h.astype(BF16)
    p1 = jnp.dot(hb, wsb_ref[...], preferred_element_type=F32)
    q_ref[...] = (p1[:, :SB_WIDTH] * (LOG2E * SB_HEAD_DIM ** -0.5)).astype(BF16)
    k_ref[...] = p1[:, SB_WIDTH:2 * SB_WIDTH]
    v_ref[...] = p1[:, 2 * SB_WIDTH:3 * SB_WIDTH]
    p2 = jnp.dot(hb, wgl_ref[...], preferred_element_type=F32)
    o = 0
    qg_ref[...] = p2[:, o:o + GLA_QK_WIDTH] * (GLA_KEY_DIM ** -0.5)
    o += GLA_QK_WIDTH
    kg_ref[...] = p2[:, o:o + GLA_QK_WIDTH]
    o += GLA_QK_WIDTH
    vg_ref[...] = p2[:, o:o + GLA_V_WIDTH].astype(BF16)
    o += GLA_V_WIDTH
    rg_ref[...] = p2[:, o:o + GLA_V_WIDTH]
    o += GLA_V_WIDTH
    a_low = p2[:, o:o + LANES]
    xg = jnp.dot(a_low.astype(BF16), wgate_ref[...], preferred_element_type=F32) + bgate_ref[...]
    la_ref[...] = (jnp.minimum(xg, 0.0) - jnp.log(1.0 + jnp.exp(-jnp.abs(xg)))) * (1.0 / GLA_GATE_TEMP)


def _mod_spec(mod, tm, rows_per_group):
    if mod.shape[1] == 1:
        tiles_per_group = rows_per_group // tm
        return pl.BlockSpec((1, 1, mod.shape[2]), lambda i: (i // tiles_per_group, 0, 0))
    return pl.BlockSpec((1, tm, mod.shape[2]), lambda i: (0, i, 0))


def _inproj(x2d, sh, sc, g_mix, w_sb, w_gl, w_gate, b_gate, tm, rows_per_group):
    m, d = x2d.shape
    row = lambda w: pl.BlockSpec((tm, w), lambda i: (i, 0))
    full = lambda a: pl.BlockSpec(a.shape, lambda i: (0,) * a.ndim)
    outs = [(SB_WIDTH, BF16), (SB_WIDTH, F32), (SB_WIDTH, F32), (GLA_QK_WIDTH, F32),
            (GLA_QK_WIDTH, F32), (GLA_V_WIDTH, BF16), (GLA_V_WIDTH, F32), (GLA_QK_WIDTH, F32)]
    return pl.pallas_call(
        _inproj_body,
        grid=(m // tm,),
        in_specs=[row(d), _mod_spec(sh, tm, rows_per_group), _mod_spec(sc, tm, rows_per_group),
                  full(g_mix), full(w_sb), full(w_gl), full(w_gate), full(b_gate)],
        out_specs=[row(w) for w, _ in outs],
        out_shape=[jax.ShapeDtypeStruct((m, w), dt) for w, dt in outs],
        compiler_params=pltpu.CompilerParams(dimension_semantics=("parallel",),
                                             vmem_limit_bytes=VMEM_LIMIT),
        name="in_proj",
    )(x2d, sh, sc, g_mix, w_sb, w_gl, w_gate, b_gate)


def _sb_prompt_body(bias_ref, q_ref, k_ref, v_ref, u_ref, o_ref,
                    mask_sc, z_sc, e_sc, cf_sc, carry_sc, acc_sc):
    bq = q_ref.shape[0]
    bk = u_ref.shape[0]
    r = bq // bk
    p = pl.program_id(1)
    i = pl.program_id(2)
    n = r * (i + 1)

    @pl.when(i == 0)
    def _():
        row = lax.broadcasted_iota(jnp.int32, (bq, bk), 0)
        col = lax.broadcasted_iota(jnp.int32, (bq, bk), 1)
        for hh in range(2):
            b = bias_ref[2 * p + hh] * LOG2E
            for t in range(r):
                mask_sc[hh, t] = jnp.where((r - 1 - t) * bk + col < row, b, SB_MASKED)
            mask_sc[hh, r] = jnp.full((bq, bk), b, F32)
            mask_sc[hh, r + 1] = jnp.full((bq, bk), SB_MASKED, F32)

    q = q_ref[...]
    lane = lax.broadcasted_iota(jnp.int32, (bq, LANES), 1)
    first_head = lane < SB_HEAD_DIM
    qh = (jnp.where(first_head, q, jnp.zeros_like(q)), jnp.where(first_head, jnp.zeros_like(q), q))
    dead = jnp.full((bq, bk), SB_MASKED, F32)
    for hh in range(2):
        z_sc[1, hh] = dead
        e_sc[0, hh] = dead.astype(BF16)
        e_sc[1, hh] = dead.astype(BF16)
        cf_sc[0, hh] = jnp.zeros((bq, LANES), F32)
        cf_sc[1, hh] = jnp.zeros((bq, LANES), F32)
        carry_sc[hh] = jnp.zeros((bq, LANES), F32)
        acc_sc[hh] = jnp.zeros((bq, LANES), F32)

    def tile_start(t):
        j = r * i + r - 1 - jnp.clip(t, 0, n - 1)
        return pl.multiple_of(j * bk, bk)

    def iteration(t, slot):
        kb = k_ref[pl.ds(tile_start(t), bk), :].astype(BF16)
        m = jnp.where(t < n, jnp.minimum(t, r), r + 1)
        for hh in range(2):
            z_sc[slot, hh] = (lax.dot_general(qh[hh], kb, _NT, preferred_element_type=F32)
                              + mask_sc[hh, m])
        for hh in range(2):
            z = z_sc[1 - slot, hh]
            c = carry_sc[hh]
            a = jnp.dot(_softplus2(z).astype(BF16), u_ref[...], preferred_element_type=F32)
            e_sc[1 - slot, hh] = (z - a).astype(BF16)
            cf_sc[1 - slot, hh] = jnp.exp2(-c)
            carry_sc[hh] = c + jnp.broadcast_to(a[:, 0:1], c.shape)
        vb = v_ref[pl.ds(tile_start(t - 2), bk), :].astype(BF16)
        for hh in range(2):
            w = jnp.exp2(e_sc[slot, hh])
            acc_sc[hh] += cf_sc[slot, hh] * jnp.dot(w, vb, preferred_element_type=F32)

    def body(tt, _):
        iteration(2 * tt, 0)
        iteration(2 * tt + 1, 1)
        return 0

    lax.fori_loop(0, (n + 3) // 2, body, 0)
    o_ref[...] = jnp.where(first_head, acc_sc[0], acc_sc[1]).astype(BF16)


def _suffix_ones(n):
    r = np.arange(n)
    return jnp.asarray((r[:, None] >= r[None, :]).astype(np.float32), dtype=BF16)


def _sb_prompt(q, k, v, b_sb):
    b, t, w = q.shape
    bq = min(SB_BQ, t)
    bk = min(SB_BK, t)
    pairs = w // LANES
    return pl.pallas_call(
        _sb_prompt_body,
        grid=(b, pairs, t // bq),
        in_specs=[pl.BlockSpec(memory_space=pltpu.SMEM),
                  pl.BlockSpec((None, bq, LANES), lambda bb, p, i: (bb, i, p)),
                  pl.BlockSpec((None, t, LANES), lambda bb, p, i: (bb, 0, p)),
                  pl.BlockSpec((None, t, LANES), lambda bb, p, i: (bb, 0, p)),
                  pl.BlockSpec((bk, bk), lambda bb, p, i: (0, 0))],
        out_specs=pl.BlockSpec((None, bq, LANES), lambda bb, p, i: (bb, i, p)),
        out_shape=jax.ShapeDtypeStruct((b, t, w), BF16),
        scratch_shapes=[pltpu.VMEM((2, bq // bk + 2, bq, bk), F32),
                        pltpu.VMEM((2, 2, bq, bk), F32),
                        pltpu.VMEM((2, 2, bq, bk), BF16),
                        pltpu.VMEM((2, 2, bq, LANES), F32),
                        pltpu.VMEM((2, bq, LANES), F32),
                        pltpu.VMEM((2, bq, LANES), F32)],
        compiler_params=pltpu.CompilerParams(
            dimension_semantics=("parallel", "parallel", "arbitrary"),
            vmem_limit_bytes=VMEM_LIMIT),
        name="sb_prompt",
    )(b_sb, q, k, v, _suffix_ones(bk))


def _sb_decode_body(pt_ref, bias_ref, q_ref, kn_ref, vn_ref, *rest):
    del pt_ref
    npg = DEC_PAGES_PER_STEP
    k_pages = rest[:npg]
    v_pages = rest[npg:2 * npg]
    u_ref, o_ref, carry_sc, acc_sc = rest[2 * npg:]
    g = pl.program_id(1)
    n_tok, width = q_ref.shape[1:]
    rows = SB_HEADS * n_tok
    page = k_pages[0].shape[2]

    row = lax.broadcasted_iota(jnp.int32, (rows, width), 0)
    lane = lax.broadcasted_iota(jnp.int32, (rows, width), 1)
    own = (row // n_tok) == (lane // SB_HEAD_DIM)
    qbd = jnp.where(own, jnp.concatenate([q_ref[0].astype(F32)] * SB_HEADS, axis=0), 0.0).astype(BF16)
    row_p = lax.broadcasted_iota(jnp.int32, (rows, page), 0)
    col_p = lax.broadcasted_iota(jnp.int32, (rows, page), 1)
    bias = jnp.zeros((rows, page), F32)
    for h in range(SB_HEADS):
        bias = jnp.where(row_p // n_tok == h, bias_ref[h] * LOG2E, bias)
    u = u_ref[...]

    @pl.when(g == 0)
    def _():
        pad = jnp.zeros((page - n_tok, width), F32)
        kb = jnp.concatenate([kn_ref[0], pad], axis=0).astype(BF16)
        vb = jnp.concatenate([vn_ref[0], pad], axis=0).astype(BF16)
        causal = col_p < (row_p % n_tok)
        z = lax.dot_general(qbd, kb, _NT, preferred_element_type=F32) + bias
        sp = jnp.where(causal, _softplus2(z), 0.0)
        a = jnp.dot(sp.astype(BF16), u, preferred_element_type=F32)
        w = jnp.where(causal, jnp.exp2(z - a), 0.0)
        carry_sc[...] = jnp.broadcast_to(a[:, 0:1], carry_sc.shape)
        acc_sc[...] = jnp.dot(w.astype(BF16), vb, preferred_element_type=F32)

    order = list(reversed(range(npg)))
    as_matrix = lambda ref: ref[...].reshape(width, page).astype(BF16)
    zs = [jnp.dot(qbd, as_matrix(k_pages[n]), preferred_element_type=F32) + bias for n in order]
    sp = _softplus2(jnp.concatenate(zs, axis=0))
    a_all = jnp.dot(sp.astype(BF16), u, preferred_element_type=F32)
    c = carry_sc[...]
    acc = acc_sc[...]
    for i, n in enumerate(order):
        a_i = a_all[i * rows:(i + 1) * rows]
        w = jnp.exp2(zs[i] - a_i - c)
        c = c + jnp.broadcast_to(a_i[:, 0:1], c.shape)
        acc = acc + lax.dot_general(w.astype(BF16), as_matrix(v_pages[n]), _NT,
                                    preferred_element_type=F32)
    carry_sc[...] = c
    acc_sc[...] = acc

    @pl.when(g == pl.num_programs(1) - 1)
    def _():
        own_acc = jnp.where(own, acc, 0.0)
        out = own_acc[0:n_tok]
        for h in range(1, SB_HEADS):
            out = out + own_acc[h * n_tok:(h + 1) * n_tok]
        o_ref[0] = out.astype(BF16)


def _sb_decode(q, k_new, v_new, cache_k, cache_v, page_table, b_sb):
    nb, n_tok, w = q.shape
    n_pages = page_table.shape[1]
    page = cache_k.shape[1]
    npg = DEC_PAGES_PER_STEP
    groups = n_pages // npg
    rows = SB_HEADS * n_tok
    cache_k = jnp.transpose(cache_k, (0, 2, 3, 1))
    cache_v = jnp.transpose(cache_v, (0, 2, 3, 1))

    def page_map(n):
        return lambda b, g, pt: (pt[b, (groups - 1 - g) * npg + n], 0, 0, 0)

    tok_spec = pl.BlockSpec((1, n_tok, w), lambda b, g, pt: (b, 0, 0))
    page_specs = [pl.BlockSpec((None, SB_HEADS, SB_HEAD_DIM, page), page_map(n)) for n in range(npg)]
    grid_spec = pltpu.PrefetchScalarGridSpec(
        num_scalar_prefetch=1,
        grid=(nb, groups),
        in_specs=[pl.BlockSpec(memory_space=pltpu.SMEM), tok_spec, tok_spec, tok_spec]
        + page_specs + page_specs
        + [pl.BlockSpec((page, page), lambda b, g, pt: (0, 0))],
        out_specs=tok_spec,
        scratch_shapes=[pltpu.VMEM((rows, LANES), F32), pltpu.VMEM((rows, w), F32)],
    )
    return pl.pallas_call(
        _sb_decode_body,
        grid_spec=grid_spec,
        out_shape=jax.ShapeDtypeStruct((nb, n_tok, w), BF16),
        compiler_params=pltpu.CompilerParams(
            dimension_semantics=("parallel", "arbitrary"), vmem_limit_bytes=VMEM_LIMIT),
        name="sb_decode",
    )(page_table, b_sb, q, k_new, v_new, *([cache_k] * npg), *([cache_v] * npg), _suffix_ones(page))


def _gla_consts(tb, chunk):
    i = np.arange(tb)[:, None]
    t = np.arange(tb)[None, :]
    mats = []
    lvl = np.zeros((tb, tb), np.float32)
    code = 1
    s = tb
    while s > chunk:
        mid = (i // s) * s + s // 2
        upper = i >= mid
        mats.append(np.where(upper & (t >= mid) & (t <= i), 1.0, 0.0)
                    + np.where(~upper & (t > i) & (t < mid), -1.0, 0.0))
        mid_t = (t // s) * s + s // 2
        lvl = np.where((i // s == t // s) & upper & (t < mid_t), code, lvl)
        code += 1
        s //= 2
    mats.append(np.where((t // chunk == i // chunk) & (t <= i), 1.0, 0.0))
    lvl = np.where((i // chunk == t // chunk) & (t <= i), code, lvl)
    mats.append(np.where(t <= i, 1.0, 0.0))
    stack = np.concatenate(mats, axis=0).astype(np.float32)
    return jnp.asarray(stack, dtype=BF16), jnp.asarray(lvl, dtype=F32), code


def _gla_body(q_ref, k_ref, v_ref, r_ref, la_ref, s0_ref, g_ref, m_ref, lvl_ref,
              o_ref, st_ref, s_sc, *, n_levels):
    tb = lvl_ref.shape[0]
    n_tok = min(q_ref.shape[1], tb)
    n_sub = max(q_ref.shape[1] // tb, 1)
    blk = pl.program_id(1)

    @pl.when(blk == 0)
    def _():
        s_sc[...] = s0_ref[0]

    def load(ref, sub, lo, width):
        x = ref[0, sub * n_tok:(sub + 1) * n_tok, lo:lo + width]
        if n_tok < tb:
            x = jnp.concatenate([x.astype(F32), jnp.zeros((tb - n_tok, width), F32)], axis=0).astype(x.dtype)
        return x

    lvl = lvl_ref[...]
    lane = lax.broadcasted_iota(jnp.int32, (tb, LANES), 1)
    first_head = lane < GLA_KEY_DIM
    row_s = lax.broadcasted_iota(jnp.int32, (2 * GLA_VAL_DIM, LANES), 0)
    lane_s = lax.broadcasted_iota(jnp.int32, (2 * GLA_VAL_DIM, LANES), 1)
    own_state = (row_s < GLA_VAL_DIM) == (lane_s < GLA_KEY_DIM)
    gain = g_ref[...]

    blocks = [(p, sub) for p in range(GLA_HEADS // 2) for sub in range(n_sub)]

    def decays(p, sub):
        la = load(la_ref, sub, p * LANES, LANES)
        hi = la.astype(BF16)
        lo = (la - hi.astype(F32)).astype(BF16)
        res = jnp.dot(m_ref[...], jnp.concatenate([hi, lo], axis=1), preferred_element_type=F32)
        dall = res[:, :LANES] + res[:, LANES:]
        return [dall[n * tb:(n + 1) * tb] for n in range(n_levels + 1)]

    def scaled(p, sub, d):
        bc = d[n_levels]
        e_tot = bc[tb - 1:tb, :]
        q = load(q_ref, sub, p * LANES, LANES)
        k = load(k_ref, sub, p * LANES, LANES)
        qs, ks = [], []
        for n in range(n_levels - 1):
            x = jnp.exp(-jnp.abs(d[n]))
            qs.append((q * x, (k * x).astype(BF16)))
        qs.append((q * jnp.exp(d[n_levels - 1]), (k * jnp.exp(-d[n_levels - 1])).astype(BF16)))
        return dict(qk=qs, q_hat=(q * jnp.exp(bc)).astype(BF16),
                    k_hat=(k * jnp.exp(e_tot - bc)).astype(BF16), decay=jnp.exp(e_tot))

    def scores(s):
        out = []
        for hh in range(2):
            head = first_head if hh == 0 else jnp.logical_not(first_head)
            sc = jnp.zeros((tb, tb), F32)
            for n, (qn, kn) in enumerate(s["qk"]):
                s_n = lax.dot_general(jnp.where(head, qn, 0.0).astype(BF16), kn, _NT,
                                      preferred_element_type=F32)
                sc = jnp.where(lvl == float(n + 1), s_n, sc)
            out.append(sc.astype(BF16))
        return out

    def values(p, sub, s, sc):
        o_intra = jnp.concatenate(
            [jnp.dot(sc[hh], load(v_ref, sub, (2 * p + hh) * GLA_VAL_DIM, GLA_VAL_DIM),
                     preferred_element_type=F32) for hh in range(2)], axis=1)
        v_pair = load(v_ref, sub, 2 * p * GLA_VAL_DIM, 2 * GLA_VAL_DIM)
        ds_t = lax.dot_general(v_pair, s["k_hat"], _TN, preferred_element_type=F32)
        return o_intra, jnp.where(own_state, ds_t, 0.0)

    ds = [decays(p, sub) for p, sub in blocks]
    ss = [scaled(p, sub, d) for (p, sub), d in zip(blocks, ds)]
    scs = [scores(s) for s in ss]
    vals = [values(p, sub, s, sc) for (p, sub), s, sc in zip(blocks, ss, scs)]

    st = None
    for (p, sub), s, (o_intra, ds_t) in zip(blocks, ss, vals):
        if sub == 0:
            st = s_sc[p]
        o = o_intra + lax.dot_general(s["q_hat"], st.astype(BF16), _NT, preferred_element_type=F32)
        rows = slice(sub * n_tok, (sub + 1) * n_tok)
        for hh in range(2):
            cols = slice((2 * p + hh) * GLA_VAL_DIM, (2 * p + hh + 1) * GLA_VAL_DIM)
            oh = _rms(o[:n_tok, hh * GLA_VAL_DIM:(hh + 1) * GLA_VAL_DIM], gain)
            o_ref[0, rows, cols] = (oh * _silu(r_ref[0, rows, cols])).astype(BF16)
        st = st * s["decay"] + ds_t
        if sub == n_sub - 1:
            s_sc[p] = st
            st_ref[0, p] = st


def _gla(qg, kg, vg, rg, la, s0_t, g_out):
    b, t, _ = qg.shape
    tb = GLA_TB
    tok = min(t, tb * GLA_STEP_BLOCKS)
    nblk = t // tok
    m_stack, lvl, n_levels = _gla_consts(tb, GLA_CHUNK if t % GLA_CHUNK == 0 else t)
    blk = lambda w: pl.BlockSpec((1, tok, w), lambda bb, j: (bb, j, 0))
    full = lambda a: pl.BlockSpec(a.shape, lambda bb, j: (0,) * a.ndim)
    st_spec = pl.BlockSpec((1,) + s0_t.shape[1:], lambda bb, j: (bb, 0, 0, 0))
    return pl.pallas_call(
        functools.partial(_gla_body, n_levels=n_levels),
        grid=(b, nblk),
        in_specs=[blk(GLA_QK_WIDTH), blk(GLA_QK_WIDTH), blk(GLA_V_WIDTH), blk(GLA_V_WIDTH),
                  blk(GLA_QK_WIDTH), st_spec, full(g_out), full(m_stack), full(lvl)],
        out_specs=[blk(GLA_V_WIDTH), st_spec],
        out_shape=[jax.ShapeDtypeStruct((b, t, GLA_V_WIDTH), BF16),
                   jax.ShapeDtypeStruct(s0_t.shape, F32)],
        scratch_shapes=[pltpu.VMEM(s0_t.shape[1:], F32)],
        compiler_params=pltpu.CompilerParams(
            dimension_semantics=("parallel", "arbitrary"), vmem_limit_bytes=VMEM_LIMIT),
        name="gla",
    )(qg, kg, vg, rg, la, s0_t, g_out, m_stack, lvl)


def _state_to_t(s):
    b = s.shape[0]
    st = jnp.swapaxes(s, 2, 3).reshape(b, GLA_HEADS // 2, 2, GLA_VAL_DIM, GLA_KEY_DIM)
    z = jnp.zeros_like(st[:, :, 0])
    top = jnp.concatenate([st[:, :, 0], z], axis=-1)
    bot = jnp.concatenate([z, st[:, :, 1]], axis=-1)
    return jnp.concatenate([top, bot], axis=2)


def _state_from_t(st):
    b = st.shape[0]
    h0 = st[:, :, :GLA_VAL_DIM, :GLA_KEY_DIM]
    h1 = st[:, :, GLA_VAL_DIM:, GLA_KEY_DIM:]
    s = jnp.stack([h0, h1], axis=2).reshape(b, GLA_HEADS, GLA_VAL_DIM, GLA_KEY_DIM)
    return jnp.swapaxes(s, 2, 3)


def _post_body(x_ref, osb_ref, og_ref, ga1_ref, sh2_ref, sc2_ref, ga2_ref, gffn_ref, gfin_ref,
               wo_ref, wup_ref, wdn_ref, y_ref):
    mix = (jnp.dot(osb_ref[...], wo_ref[:SB_WIDTH, :], preferred_element_type=F32)
           + jnp.dot(og_ref[...], wo_ref[SB_WIDTH:, :], preferred_element_type=F32))
    x1 = x_ref[...] + ga1_ref[0] * mix
    h2 = (_rms(x1, gffn_ref[...]) * (1.0 + sc2_ref[0]) + sh2_ref[0]).astype(BF16)
    ff = jnp.zeros_like(x1)
    for c in range(D_FF // FF_CHUNK):
        u = jnp.dot(h2, wup_ref[:, c * FF_CHUNK:(c + 1) * FF_CHUNK], preferred_element_type=F32)
        a = jnp.square(jnp.maximum(u, 0.0)).astype(BF16)
        ff = ff + jnp.dot(a, wdn_ref[c * FF_CHUNK:(c + 1) * FF_CHUNK, :], preferred_element_type=F32)
    x2 = x1 + ga2_ref[0] * ff
    y_ref[...] = _rms(x2, gfin_ref[...])


def _post(x2d, o_sb, o_g, ga1, sh2, sc2, ga2, g_ffn, g_final, w_out, w_up, w_down, tm, rows_per_group):
    m, d = x2d.shape
    row = lambda w: pl.BlockSpec((tm, w), lambda i: (i, 0))
    full = lambda a: pl.BlockSpec(a.shape, lambda i: (0,) * a.ndim)
    resident = lambda a: pl.BlockSpec(a.shape, lambda i: (0,) * a.ndim, pipeline_mode=pl.Buffered(1))
    mod = lambda a: _mod_spec(a, tm, rows_per_group)
    return pl.pallas_call(
        _post_body,
        grid=(m // tm,),
        in_specs=[row(d), row(SB_WIDTH), row(GLA_V_WIDTH), mod(ga1), mod(sh2), mod(sc2), mod(ga2),
                  full(g_ffn), full(g_final), resident(w_out), resident(w_up), resident(w_down)],
        out_specs=row(d),
        out_shape=jax.ShapeDtypeStruct((m, d), F32),
        compiler_params=pltpu.CompilerParams(dimension_semantics=("parallel",),
                                             vmem_limit_bytes=VMEM_LIMIT),
        name="post_ffn",
    )(x2d, o_sb, o_g, ga1, sh2, sc2, ga2, g_ffn, g_final, w_out, w_up, w_down)


def _prep_weights(w_in, w_gla_gate, b_gla_gate, w_out, w_up, w_down):
    n_sb = 3 * SB_WIDTH
    n_gl = 2 * GLA_QK_WIDTH + 2 * GLA_V_WIDTH
    w_sb = w_in[:, :n_sb].astype(BF16)
    a_pad = jnp.pad(w_in[:, n_sb + n_gl:], ((0, 0), (0, LANES - GLA_GATE_RANK)))
    w_gl = jnp.concatenate([w_in[:, n_sb:n_sb + n_gl], a_pad], axis=1).astype(BF16)
    w_gate = jnp.pad(w_gla_gate, ((0, LANES - GLA_GATE_RANK), (0, 0))).astype(BF16)
    return (w_sb, w_gl, w_gate, b_gla_gate.reshape(1, -1),
            w_out.astype(BF16), w_up.astype(BF16), w_down.astype(BF16))


def _layer(x, mods, weights, norms, sb_fn, s0_t, tm):
    g_rows, t, d = x.shape
    sh1, sc1, ga1, sh2, sc2, ga2 = mods
    w_sb, w_gl, w_gate, b_gate, w_out, w_up, w_down = weights
    g_mix, g_gla_out, g_ffn, g_final = norms
    x2d = x.reshape(g_rows * t, d)
    rows_per_group = t if sh1.shape[1] == 1 else g_rows * t
    q, k, v, qg, kg, vg, rg, la = _inproj(x2d, sh1, sc1, g_mix, w_sb, w_gl, w_gate, b_gate,
                                          tm, rows_per_group)
    r3 = lambda a: a.reshape(g_rows, t, a.shape[-1])
    o_sb = sb_fn(r3(q), r3(k), r3(v))
    o_g, st = _gla(r3(qg), r3(kg), r3(vg), r3(rg), r3(la), s0_t, g_gla_out)
    y = _post(x2d, o_sb.reshape(g_rows * t, -1), o_g.reshape(g_rows * t, -1), ga1, sh2, sc2, ga2,
              g_ffn, g_final, w_out, w_up, w_down, tm, rows_per_group)
    return (y.reshape(g_rows, t, d), k.reshape(g_rows, t, SB_HEADS, SB_HEAD_DIM),
            v.reshape(g_rows, t, SB_HEADS, SB_HEAD_DIM), _state_from_t(st))


def kernel(x_prompt, x_sample, c_prompt, c_sample, cache_k, cache_v, state_gla, page_table, w_ada, b_ada, g_mix, w_in, b_sb, w_gla_gate, b_gla_gate, g_gla_out, w_out, g_ffn, w_up, w_down, g_final):
    depth = w_ada.shape[0]
    assert depth == 1, "final-norm fusion assumes a single layer"
    nb, t, d = x_prompt.shape
    nd, td, _ = x_sample.shape
    l = 0

    c_all = jnp.concatenate([c_prompt, c_sample], axis=0)
    pad_rows = (-c_all.shape[0]) % SUBLANES
    mod = _ada(jnp.pad(c_all, ((0, pad_rows), (0, 0))), w_ada[l], b_ada[l])
    mods_p = [m[:nb].reshape(nb, 1, d) for m in jnp.split(mod, 6, axis=-1)]
    mods_s = [jnp.repeat(m[nb:nb + nd], td, axis=0).reshape(1, nd * td, d)
              for m in jnp.split(mod, 6, axis=-1)]

    weights = _prep_weights(w_in[l], w_gla_gate[l], b_gla_gate[l], w_out[l], w_up[l], w_down[l])
    norms = (g_mix[l].reshape(1, d), g_gla_out[l].reshape(1, -1), g_ffn[l].reshape(1, d),
             g_final.reshape(1, d))

    sb_p = lambda q, k, v: _sb_prompt(q, k, v, b_sb[l])
    zeros_t = jnp.zeros((nb, GLA_HEADS // 2, 2 * GLA_VAL_DIM, 2 * GLA_KEY_DIM), F32)
    y_p, k_p, v_p, s_p = _layer(x_prompt, mods_p, weights, norms, sb_p, zeros_t, min(ROW_TILE, t))

    sb_s = lambda q, k, v: _sb_decode(q, k, v, cache_k[l], cache_v[l], page_table, b_sb[l])
    y_s, k_s, v_s, s_s = _layer(x_sample, mods_s, weights, norms, sb_s, _state_to_t(state_gla[l]),
                                nd * td)

    return (y_p, y_s, k_p[None], v_p[None], s_p[None], k_s[None], v_s[None], s_s[None])
```

```python
import functools

import numpy as np
import jax
import jax.numpy as jnp
from jax import lax
from jax.experimental import pallas as pl
from jax.experimental.pallas import tpu as pltpu

F32 = jnp.float32
BF16 = jnp.bfloat16

D_MODEL = 1024
SB_HEADS = 8
SB_HEAD_DIM = 64
SB_WIDTH = SB_HEADS * SB_HEAD_DIM
GLA_HEADS = 4
GLA_KEY_DIM = 64
GLA_VAL_DIM = 128
GLA_QK_WIDTH = GLA_HEADS * GLA_KEY_DIM
GLA_V_WIDTH = GLA_HEADS * GLA_VAL_DIM
GLA_GATE_RANK = 16
GLA_GATE_TEMP = 16.0
GLA_CHUNK = 16
D_FF = 4 * D_MODEL
NORM_EPS = 1e-6

LANES = 128
SUBLANES = 8
VMEM_LIMIT = 56 * 1024 * 1024

SB_BQ = 512
SB_BK = 256
GLA_TB = 128
GLA_STEP_BLOCKS = 4
DEC_PAGES_PER_STEP = 16
ROW_TILE = 512
ADA_TN = 1024
FF_CHUNK = 1024
SB_MASKED = -1e30

_NT = (((1,), (1,)), ((), ()))
_TN = (((0,), (0,)), ((), ()))


LOG2E = float(np.log2(np.e))


def _softplus2(z2):
    return jnp.maximum(z2, 0.0) + jnp.log(1.0 + jnp.exp2(-jnp.abs(z2))) * LOG2E


def _silu(x):
    return x * jax.nn.sigmoid(x)


def _rms(x, gain):
    ms = jnp.mean(x * x, axis=-1, keepdims=True)
    return x * lax.rsqrt(ms + NORM_EPS) * gain


def _ada_body(c_ref, w_ref, b_ref, o_ref):
    s = _silu(c_ref[...])
    o_ref[...] = jnp.dot(s.astype(BF16), w_ref[...].astype(BF16),
                         preferred_element_type=F32) + b_ref[...]


def _ada(c, w_ada, b_ada):
    rows, d = c.shape
    n = w_ada.shape[1]
    return pl.pallas_call(
        _ada_body,
        grid=(n // ADA_TN,),
        in_specs=[pl.BlockSpec((rows, d), lambda j: (0, 0)),
                  pl.BlockSpec((d, ADA_TN), lambda j: (0, j)),
                  pl.BlockSpec((1, ADA_TN), lambda j: (0, j))],
        out_specs=pl.BlockSpec((rows, ADA_TN), lambda j: (0, j)),
        out_shape=jax.ShapeDtypeStruct((rows, n), F32),
        name="ada_mod",
    )(c, w_ada, b_ada.reshape(1, n))


def _rows_to_heads(x):
    rows = x.shape[0]
    assert SB_HEADS == SUBLANES
    groups = rows // SUBLANES
    t = [x[:, h * SB_HEAD_DIM:(h + 1) * SB_HEAD_DIM].reshape(groups, SUBLANES, SB_HEAD_DIM)
         for h in range(SB_HEADS)]
    sub = lax.broadcasted_iota(jnp.int32, t[0].shape, 1)
    for dist in (4, 2, 1):
        keep = (sub & dist) == 0
        y = list(t)
        for v in range(SB_HEADS):
            if v & dist == 0:
                a, b = t[v], t[v + dist]
                y[v] = jnp.where(keep, a, pltpu.roll(b, dist, axis=1))
                y[v + dist] = jnp.where(keep, pltpu.roll(a, SUBLANES - dist, axis=1), b)
        t = y
    return jnp.stack(t, axis=1).reshape(rows, SB_HEADS, SB_HEAD_DIM)


def _inproj_body(x_ref, sh_ref, sc_ref, g_ref, wsb_ref, wgl_ref, wgate_ref, bgate_ref,
                 q_ref, k_ref, v_ref, kh_ref, vh_ref, qg_ref, kg_ref, vg_ref, rg_ref, la_ref):
    h = _rms(x_ref[...], g_ref[...]) * (1.0 + sc_ref[0]) + sh_ref[0]
    hb = h.astype(BF16)
    p1 = jnp.dot(hb, wsb_ref[...], preferred_element_type=F32)
    q_ref[...] = (p1[:, :SB_WIDTH] * (LOG2E * SB_HEAD_DIM ** -0.5)).astype(BF16)
    k = p1[:, SB_WIDTH:2 * SB_WIDTH]
    v = p1[:, 2 * SB_WIDTH:3 * SB_WIDTH]
    k_ref[...] = k.astype(BF16)
    v_ref[...] = v.astype(BF16)
    kh_ref[...] = _rows_to_heads(k)
    vh_ref[...] = _rows_to_heads(v)
    p2 = jnp.dot(hb, wgl_ref[...], preferred_element_type=F32)
    o = 0
    qg_ref[...] = p2[:, o:o + GLA_QK_WIDTH] * (GLA_KEY_DIM ** -0.5)
    o += GLA_QK_WIDTH
    kg_ref[...] = p2[:, o:o + GLA_QK_WIDTH]
    o += GLA_QK_WIDTH
    vg_ref[...] = p2[:, o:o + GLA_V_WIDTH].astype(BF16)
    o += GLA_V_WIDTH
    rg_ref[...] = p2[:, o:o + GLA_V_WIDTH]
    o += GLA_V_WIDTH
    a_low = p2[:, o:o + LANES]
    xg = jnp.dot(a_low.astype(BF16), wgate_ref[...], preferred_element_type=F32) + bgate_ref[...]
    la_ref[...] = (jnp.minimum(xg, 0.0) - jnp.log(1.0 + jnp.exp(-jnp.abs(xg)))) * (1.0 / GLA_GATE_TEMP)


def _mod_spec(mod, tm, rows_per_group):
    if mod.shape[1] == 1:
        tiles_per_group = rows_per_group // tm
        return pl.BlockSpec((1, 1, mod.shape[2]), lambda i: (i // tiles_per_group, 0, 0))
    return pl.BlockSpec((1, tm, mod.shape[2]), lambda i: (0, i, 0))


def _inproj(x2d, sh, sc, g_mix, w_sb, w_gl, w_gate, b_gate, tm, rows_per_group):
    m, d = x2d.shape
    row = lambda w: pl.BlockSpec((tm, w), lambda i: (i, 0))
    full = lambda a: pl.BlockSpec(a.shape, lambda i: (0,) * a.ndim)
    heads = ((SB_HEADS, SB_HEAD_DIM), F32)
    outs = [(SB_WIDTH, BF16), (SB_WIDTH, BF16), (SB_WIDTH, BF16), heads, heads, (GLA_QK_WIDTH, F32),
            (GLA_QK_WIDTH, F32), (GLA_V_WIDTH, BF16), (GLA_V_WIDTH, F32), (GLA_QK_WIDTH, F32)]
    tail = lambda w: w if isinstance(w, tuple) else (w,)
    out_spec = lambda w: pl.BlockSpec((tm,) + tail(w), lambda i: (i,) + (0,) * len(tail(w)))
    return pl.pallas_call(
        _inproj_body,
        grid=(m // tm,),
        in_specs=[row(d), _mod_spec(sh, tm, rows_per_group), _mod_spec(sc, tm, rows_per_group),
                  full(g_mix), full(w_sb), full(w_gl), full(w_gate), full(b_gate)],
        out_specs=[out_spec(w) for w, _ in outs],
        out_shape=[jax.ShapeDtypeStruct((m,) + tail(w), dt) for w, dt in outs],
        compiler_params=pltpu.CompilerParams(dimension_semantics=("parallel",),
                                             vmem_limit_bytes=VMEM_LIMIT),
        name="in_proj",
    )(x2d, sh, sc, g_mix, w_sb, w_gl, w_gate, b_gate)


def _sb_prompt_body(bias_ref, q_ref, k_ref, v_ref, u_ref, o_ref,
                    mask_sc, z_sc, e_sc, cf_sc, carry_sc, acc_sc):
    bq = q_ref.shape[0]
    bk = u_ref.shape[0]
    r = bq // bk
    p = pl.program_id(1)
    i = pl.program_id(2)
    n = r * (i + 1)

    @pl.when(i == 0)
    def _():
        row = lax.broadcasted_iota(jnp.int32, (bq, bk), 0)
        col = lax.broadcasted_iota(jnp.int32, (bq, bk), 1)
        for hh in range(2):
            b = bias_ref[2 * p + hh] * LOG2E
            for t in range(r):
                mask_sc[hh, t] = jnp.where((r - 1 - t) * bk + col < row, b, SB_MASKED)
            mask_sc[hh, r] = jnp.full((bq, bk), b, F32)
            mask_sc[hh, r + 1] = jnp.full((bq, bk), SB_MASKED, F32)

    q = q_ref[...]
    lane = lax.broadcasted_iota(jnp.int32, (bq, LANES), 1)
    first_head = lane < SB_HEAD_DIM
    qh = (jnp.where(first_head, q, jnp.zeros_like(q)), jnp.where(first_head, jnp.zeros_like(q), q))
    dead = jnp.full((bq, bk), SB_MASKED, F32)
    for hh in range(2):
        z_sc[1, hh] = dead
        e_sc[0, hh] = dead.astype(BF16)
        cf_sc[0, hh] = jnp.zeros((bq, LANES), F32)
        carry_sc[hh] = jnp.zeros((bq, LANES), F32)
        acc_sc[hh] = jnp.zeros((bq, LANES), F32)

    def tile_start(t):
        j = r * i + r - 1 - jnp.clip(t, 0, n - 1)
        return pl.multiple_of(j * bk, bk)

    def iteration(t, slot):
        kb = k_ref[pl.ds(tile_start(t), bk), :]
        m = jnp.where(t < n, jnp.minimum(t, r), r + 1)
        for hh in range(2):
            z_sc[slot, hh] = (lax.dot_general(qh[hh], kb, _NT, preferred_element_type=F32)
                              + mask_sc[hh, m])
        for hh in range(2):
            z = z_sc[1 - slot, hh]
            c = carry_sc[hh]
            a = jnp.dot(_softplus2(z).astype(BF16), u_ref[...], preferred_element_type=F32)
            e_sc[1 - slot, hh] = (z - a).astype(BF16)
            cf_sc[1 - slot, hh] = jnp.exp2(-c)
            carry_sc[hh] = c + jnp.broadcast_to(a[:, 0:1], c.shape)
        vb = v_ref[pl.ds(tile_start(t - 2), bk), :]
        for hh in range(2):
            w = jnp.exp2(e_sc[slot, hh])
            acc_sc[hh] += cf_sc[slot, hh] * jnp.dot(w, vb, preferred_element_type=F32)

    def body(tt, _):
        for s in range(4):
            iteration(4 * tt + s, s % 2)
        return 0

    total = n + 2 + (n % 2)
    lax.fori_loop(0, total // 4, body, 0)

    @pl.when(total % 4 != 0)
    def _():
        iteration(total - 2, 0)
        iteration(total - 1, 1)
    o_ref[...] = jnp.where(first_head, acc_sc[0], acc_sc[1]).astype(BF16)


def _suffix_ones(n):
    r = np.arange(n)
    return jnp.asarray((r[:, None] >= r[None, :]).astype(np.float32), dtype=BF16)


def _sb_prompt(q, k, v, b_sb):
    b, t, w = q.shape
    bq = min(SB_BQ, t)
    bk = min(SB_BK, t)
    pairs = w // LANES
    return pl.pallas_call(
        _sb_prompt_body,
        grid=(b, pairs, t // bq),
        in_specs=[pl.BlockSpec(memory_space=pltpu.SMEM),
                  pl.BlockSpec((None, bq, LANES), lambda bb, p, i: (bb, i, p)),
                  pl.BlockSpec((None, t, LANES), lambda bb, p, i: (bb, 0, p)),
                  pl.BlockSpec((None, t, LANES), lambda bb, p, i: (bb, 0, p)),
                  pl.BlockSpec((bk, bk), lambda bb, p, i: (0, 0))],
        out_specs=pl.BlockSpec((None, bq, LANES), lambda bb, p, i: (bb, i, p)),
        out_shape=jax.ShapeDtypeStruct((b, t, w), BF16),
        scratch_shapes=[pltpu.VMEM((2, bq // bk + 2, bq, bk), F32),
                        pltpu.VMEM((2, 2, bq, bk), F32),
                        pltpu.VMEM((2, 2, bq, bk), BF16),
                        pltpu.VMEM((2, 2, bq, LANES), F32),
                        pltpu.VMEM((2, bq, LANES), F32),
                        pltpu.VMEM((2, bq, LANES), F32)],
        compiler_params=pltpu.CompilerParams(
            dimension_semantics=("parallel", "parallel", "arbitrary"),
            vmem_limit_bytes=VMEM_LIMIT),
        name="sb_prompt",
    )(b_sb, q, k, v, _suffix_ones(bk))


def _sb_decode_body(pt_ref, bias_ref, q_ref, kn_ref, vn_ref, *rest):
    del pt_ref
    npg = DEC_PAGES_PER_STEP
    k_pages = rest[:npg]
    v_pages = rest[npg:2 * npg]
    u_ref, o_ref, carry_sc, acc_sc = rest[2 * npg:]
    g = pl.program_id(1)
    n_tok, width = q_ref.shape[1:]
    rows = SB_HEADS * n_tok
    page = k_pages[0].shape[2]

    row = lax.broadcasted_iota(jnp.int32, (rows, width), 0)
    lane = lax.broadcasted_iota(jnp.int32, (rows, width), 1)
    own = (row // n_tok) == (lane // SB_HEAD_DIM)
    qbd = jnp.where(own, jnp.concatenate([q_ref[0].astype(F32)] * SB_HEADS, axis=0), 0.0).astype(BF16)
    row_p = lax.broadcasted_iota(jnp.int32, (rows, page), 0)
    col_p = lax.broadcasted_iota(jnp.int32, (rows, page), 1)
    bias = jnp.zeros((rows, page), F32)
    for h in range(SB_HEADS):
        bias = jnp.where(row_p // n_tok == h, bias_ref[h] * LOG2E, bias)
    u = u_ref[...]

    @pl.when(g == 0)
    def _():
        pad = jnp.zeros((page - n_tok, width), F32)
        kb = jnp.concatenate([kn_ref[0].astype(F32), pad], axis=0).astype(BF16)
        vb = jnp.concatenate([vn_ref[0].astype(F32), pad], axis=0).astype(BF16)
        causal = col_p < (row_p % n_tok)
        z = lax.dot_general(qbd, kb, _NT, preferred_element_type=F32) + bias
        sp = jnp.where(causal, _softplus2(z), 0.0)
        a = jnp.dot(sp.astype(BF16), u, preferred_element_type=F32)
        w = jnp.where(causal, jnp.exp2(z - a), 0.0)
        carry_sc[...] = jnp.broadcast_to(a[:, 0:1], carry_sc.shape)
        acc_sc[...] = jnp.dot(w.astype(BF16), vb, preferred_element_type=F32)

    order = list(reversed(range(npg)))
    as_matrix = lambda ref: ref[...].reshape(width, page).astype(BF16)
    zs = [jnp.dot(qbd, as_matrix(k_pages[n]), preferred_element_type=F32) + bias for n in order]
    sp = _softplus2(jnp.concatenate(zs, axis=0))
    a_all = jnp.dot(sp.astype(BF16), u, preferred_element_type=F32)
    c = carry_sc[...]
    acc = acc_sc[...]
    for i, n in enumerate(order):
        a_i = a_all[i * rows:(i + 1) * rows]
        w = jnp.exp2(zs[i] - a_i - c)
        c = c + jnp.broadcast_to(a_i[:, 0:1], c.shape)
        acc = acc + lax.dot_general(w.astype(BF16), as_matrix(v_pages[n]), _NT,
                                    preferred_element_type=F32)
    carry_sc[...] = c
    acc_sc[...] = acc

    @pl.when(g == pl.num_programs(1) - 1)
    def _():
        own_acc = jnp.where(own, acc, 0.0)
        out = own_acc[0:n_tok]
        for h in range(1, SB_HEADS):
            out = out + own_acc[h * n_tok:(h + 1) * n_tok]
        o_ref[0] = out.astype(BF16)


def _sb_decode(q, k_new, v_new, cache_k, cache_v, page_table, b_sb):
    nb, n_tok, w = q.shape
    n_pages = page_table.shape[1]
    page = cache_k.shape[1]
    npg = DEC_PAGES_PER_STEP
    groups = n_pages // npg
    rows = SB_HEADS * n_tok
    cache_k = jnp.transpose(cache_k, (0, 2, 3, 1))
    cache_v = jnp.transpose(cache_v, (0, 2, 3, 1))

    def page_map(n):
        return lambda b, g, pt: (pt[b, (groups - 1 - g) * npg + n], 0, 0, 0)

    tok_spec = pl.BlockSpec((1, n_tok, w), lambda b, g, pt: (b, 0, 0))
    page_specs = [pl.BlockSpec((None, SB_HEADS, SB_HEAD_DIM, page), page_map(n)) for n in range(npg)]
    grid_spec = pltpu.PrefetchScalarGridSpec(
        num_scalar_prefetch=1,
        grid=(nb, groups),
        in_specs=[pl.BlockSpec(memory_space=pltpu.SMEM), tok_spec, tok_spec, tok_spec]
        + page_specs + page_specs
        + [pl.BlockSpec((page, page), lambda b, g, pt: (0, 0))],
        out_specs=tok_spec,
        scratch_shapes=[pltpu.VMEM((rows, LANES), F32), pltpu.VMEM((rows, w), F32)],
    )
    return pl.pallas_call(
        _sb_decode_body,
        grid_spec=grid_spec,
        out_shape=jax.ShapeDtypeStruct((nb, n_tok, w), BF16),
        compiler_params=pltpu.CompilerParams(
            dimension_semantics=("parallel", "arbitrary"), vmem_limit_bytes=VMEM_LIMIT),
        name="sb_decode",
    )(page_table, b_sb, q, k_new, v_new, *([cache_k] * npg), *([cache_v] * npg), _suffix_ones(page))


def _gla_consts(tb, chunk):
    i = np.arange(tb)[:, None]
    t = np.arange(tb)[None, :]
    mats = []
    lvl = np.zeros((tb, tb), np.float32)
    code = 1
    s = tb
    while s > chunk:
        mid = (i // s) * s + s // 2
        upper = i >= mid
        mats.append(np.where(upper & (t >= mid) & (t <= i), 1.0, 0.0)
                    + np.where(~upper & (t > i) & (t < mid), -1.0, 0.0))
        mid_t = (t // s) * s + s // 2
        lvl = np.where((i // s == t // s) & upper & (t < mid_t), code, lvl)
        code += 1
        s //= 2
    mats.append(np.where((t // chunk == i // chunk) & (t <= i), 1.0, 0.0))
    lvl = np.where((i // chunk == t // chunk) & (t <= i), code, lvl)
    mats.append(np.where(t <= i, 1.0, 0.0))
    stack = np.concatenate(mats, axis=0).astype(np.float32)
    return jnp.asarray(stack, dtype=BF16), jnp.asarray(lvl, dtype=F32), code


def _gla_body(q_ref, k_ref, v_ref, r_ref, la_ref, s0_ref, g_ref, m_ref, lvl_ref,
              o_ref, st_ref, s_sc, *, n_levels):
    tb = lvl_ref.shape[0]
    n_tok = min(q_ref.shape[1], tb)
    n_sub = max(q_ref.shape[1] // tb, 1)
    blk = pl.program_id(1)

    @pl.when(blk == 0)
    def _():
        s_sc[...] = s0_ref[0]

    def load(ref, sub, lo, width):
        x = ref[0, sub * n_tok:(sub + 1) * n_tok, lo:lo + width]
        if n_tok < tb:
            x = jnp.concatenate([x.astype(F32), jnp.zeros((tb - n_tok, width), F32)], axis=0).astype(x.dtype)
        return x

    lvl = lvl_ref[...]
    lane = lax.broadcasted_iota(jnp.int32, (tb, LANES), 1)
    first_head = lane < GLA_KEY_DIM
    row_s = lax.broadcasted_iota(jnp.int32, (2 * GLA_VAL_DIM, LANES), 0)
    lane_s = lax.broadcasted_iota(jnp.int32, (2 * GLA_VAL_DIM, LANES), 1)
    own_state = (row_s < GLA_VAL_DIM) == (lane_s < GLA_KEY_DIM)
    gain = g_ref[...]

    blocks = [(p, sub) for p in range(GLA_HEADS // 2) for sub in range(n_sub)]

    def decays(p, sub):
        la = load(la_ref, sub, p * LANES, LANES)
        hi = la.astype(BF16)
        lo = (la - hi.astype(F32)).astype(BF16)
        res = jnp.dot(m_ref[...], jnp.concatenate([hi, lo], axis=1), preferred_element_type=F32)
        dall = res[:, :LANES] + res[:, LANES:]
        return [dall[n * tb:(n + 1) * tb] for n in range(n_levels + 1)]

    def scaled(p, sub, d):
        bc = d[n_levels]
        e_tot = bc[tb - 1:tb, :]
        q = load(q_ref, sub, p * LANES, LANES)
        k = load(k_ref, sub, p * LANES, LANES)
        qs, ks = [], []
        for n in range(n_levels - 1):
            x = jnp.exp(-jnp.abs(d[n]))
            qs.append((q * x, (k * x).astype(BF16)))
        qs.append((q * jnp.exp(d[n_levels - 1]), (k * jnp.exp(-d[n_levels - 1])).astype(BF16)))
        return dict(qk=qs, q_hat=(q * jnp.exp(bc)).astype(BF16),
                    k_hat=(k * jnp.exp(e_tot - bc)).astype(BF16), decay=jnp.exp(e_tot))

    def scores(s):
        out = []
        for hh in range(2):
            head = first_head if hh == 0 else jnp.logical_not(first_head)
            sc = jnp.zeros((tb, tb), F32)
            for n, (qn, kn) in enumerate(s["qk"]):
                s_n = lax.dot_general(jnp.where(head, qn, 0.0).astype(BF16), kn, _NT,
                                      preferred_element_type=F32)
                sc = jnp.where(lvl == float(n + 1), s_n, sc)
            out.append(sc.astype(BF16))
        return out

    def values(p, sub, s, sc):
        o_intra = jnp.concatenate(
            [jnp.dot(sc[hh], load(v_ref, sub, (2 * p + hh) * GLA_VAL_DIM, GLA_VAL_DIM),
                     preferred_element_type=F32) for hh in range(2)], axis=1)
        v_pair = load(v_ref, sub, 2 * p * GLA_VAL_DIM, 2 * GLA_VAL_DIM)
        ds_t = lax.dot_general(v_pair, s["k_hat"], _TN, preferred_element_type=F32)
        return o_intra, jnp.where(own_state, ds_t, 0.0)

    ds = [decays(p, sub) for p, sub in blocks]
    ss = [scaled(p, sub, d) for (p, sub), d in zip(blocks, ds)]
    scs = [scores(s) for s in ss]
    vals = [values(p, sub, s, sc) for (p, sub), s, sc in zip(blocks, ss, scs)]

    st = None
    for (p, sub), s, (o_intra, ds_t) in zip(blocks, ss, vals):
        if sub == 0:
            st = s_sc[p]
        o = o_intra + lax.dot_general(s["q_hat"], st.astype(BF16), _NT, preferred_element_type=F32)
        rows = slice(sub * n_tok, (sub + 1) * n_tok)
        for hh in range(2):
            cols = slice((2 * p + hh) * GLA_VAL_DIM, (2 * p + hh + 1) * GLA_VAL_DIM)
            oh = _rms(o[:n_tok, hh * GLA_VAL_DIM:(hh + 1) * GLA_VAL_DIM], gain)
            o_ref[0, rows, cols] = (oh * _silu(r_ref[0, rows, cols])).astype(BF16)
        st = st * s["decay"] + ds_t
        if sub == n_sub - 1:
            s_sc[p] = st
            st_ref[0, p] = st


def _gla(qg, kg, vg, rg, la, s0_t, g_out):
    b, t, _ = qg.shape
    tb = GLA_TB
    tok = min(t, tb * GLA_STEP_BLOCKS)
    nblk = t // tok
    m_stack, lvl, n_levels = _gla_consts(tb, GLA_CHUNK if t % GLA_CHUNK == 0 else t)
    blk = lambda w: pl.BlockSpec((1, tok, w), lambda bb, j: (bb, j, 0))
    full = lambda a: pl.BlockSpec(a.shape, lambda bb, j: (0,) * a.ndim)
    st_spec = pl.BlockSpec((1,) + s0_t.shape[1:], lambda bb, j: (bb, 0, 0, 0))
    return pl.pallas_call(
        functools.partial(_gla_body, n_levels=n_levels),
        grid=(b, nblk),
        in_specs=[blk(GLA_QK_WIDTH), blk(GLA_QK_WIDTH), blk(GLA_V_WIDTH), blk(GLA_V_WIDTH),
                  blk(GLA_QK_WIDTH), st_spec, full(g_out), full(m_stack), full(lvl)],
        out_specs=[blk(GLA_V_WIDTH), st_spec],
        out_shape=[jax.ShapeDtypeStruct((b, t, GLA_V_WIDTH), BF16),
                   jax.ShapeDtypeStruct(s0_t.shape, F32)],
        scratch_shapes=[pltpu.VMEM(s0_t.shape[1:], F32)],
        compiler_params=pltpu.CompilerParams(
            dimension_semantics=("parallel", "arbitrary"), vmem_limit_bytes=VMEM_LIMIT),
        name="gla",
    )(qg, kg, vg, rg, la, s0_t, g_out, m_stack, lvl)


def _state_to_t(s):
    b = s.shape[0]
    st = jnp.swapaxes(s, 2, 3).reshape(b, GLA_HEADS // 2, 2, GLA_VAL_DIM, GLA_KEY_DIM)
    z = jnp.zeros_like(st[:, :, 0])
    top = jnp.concatenate([st[:, :, 0], z], axis=-1)
    bot = jnp.concatenate([z, st[:, :, 1]], axis=-1)
    return jnp.concatenate([top, bot], axis=2)


def _state_from_t(st):
    b = st.shape[0]
    h0 = st[:, :, :GLA_VAL_DIM, :GLA_KEY_DIM]
    h1 = st[:, :, GLA_VAL_DIM:, GLA_KEY_DIM:]
    s = jnp.stack([h0, h1], axis=2).reshape(b, GLA_HEADS, GLA_VAL_DIM, GLA_KEY_DIM)
    return jnp.swapaxes(s, 2, 3)


def _post_body(x_ref, osb_ref, og_ref, ga1_ref, sh2_ref, sc2_ref, ga2_ref, gffn_ref, gfin_ref,
               wo_ref, wup_ref, wdn_ref, y_ref):
    mix = (jnp.dot(osb_ref[...], wo_ref[:SB_WIDTH, :], preferred_element_type=F32)
           + jnp.dot(og_ref[...], wo_ref[SB_WIDTH:, :], preferred_element_type=F32))
    x1 = x_ref[...] + ga1_ref[0] * mix
    h2 = (_rms(x1, gffn_ref[...]) * (1.0 + sc2_ref[0]) + sh2_ref[0]).astype(BF16)
    ff = jnp.zeros_like(x1)
    for c in range(D_FF // FF_CHUNK):
        u = jnp.dot(h2, wup_ref[:, c * FF_CHUNK:(c + 1) * FF_CHUNK], preferred_element_type=F32)
        a = jnp.square(jnp.maximum(u, 0.0)).astype(BF16)
        ff = ff + jnp.dot(a, wdn_ref[c * FF_CHUNK:(c + 1) * FF_CHUNK, :], preferred_element_type=F32)
    x2 = x1 + ga2_ref[0] * ff
    y_ref[...] = _rms(x2, gfin_ref[...])


def _post(x2d, o_sb, o_g, ga1, sh2, sc2, ga2, g_ffn, g_final, w_out, w_up, w_down, tm, rows_per_group):
    m, d = x2d.shape
    row = lambda w: pl.BlockSpec((tm, w), lambda i: (i, 0))
    full = lambda a: pl.BlockSpec(a.shape, lambda i: (0,) * a.ndim)
    resident = lambda a: pl.BlockSpec(a.shape, lambda i: (0,) * a.ndim, pipeline_mode=pl.Buffered(1))
    mod = lambda a: _mod_spec(a, tm, rows_per_group)
    return pl.pallas_call(
        _post_body,
        grid=(m // tm,),
        in_specs=[row(d), row(SB_WIDTH), row(GLA_V_WIDTH), mod(ga1), mod(sh2), mod(sc2), mod(ga2),
                  full(g_ffn), full(g_final), resident(w_out), resident(w_up), resident(w_down)],
        out_specs=row(d),
        out_shape=jax.ShapeDtypeStruct((m, d), F32),
        compiler_params=pltpu.CompilerParams(dimension_semantics=("parallel",),
                                             vmem_limit_bytes=VMEM_LIMIT),
        name="post_ffn",
    )(x2d, o_sb, o_g, ga1, sh2, sc2, ga2, g_ffn, g_final, w_out, w_up, w_down)


def _prep_weights(w_in, w_gla_gate, b_gla_gate, w_out, w_up, w_down):
    n_sb = 3 * SB_WIDTH
    n_gl = 2 * GLA_QK_WIDTH + 2 * GLA_V_WIDTH
    w_sb = w_in[:, :n_sb].astype(BF16)
    a_pad = jnp.pad(w_in[:, n_sb + n_gl:], ((0, 0), (0, LANES - GLA_GATE_RANK)))
    w_gl = jnp.concatenate([w_in[:, n_sb:n_sb + n_gl], a_pad], axis=1).astype(BF16)
    w_gate = jnp.pad(w_gla_gate, ((0, LANES - GLA_GATE_RANK), (0, 0))).astype(BF16)
    return (w_sb, w_gl, w_gate, b_gla_gate.reshape(1, -1),
            w_out.astype(BF16), w_up.astype(BF16), w_down.astype(BF16))


def _layer(x, mods, weights, norms, sb_fn, s0_t, tm):
    g_rows, t, d = x.shape
    sh1, sc1, ga1, sh2, sc2, ga2 = mods
    w_sb, w_gl, w_gate, b_gate, w_out, w_up, w_down = weights
    g_mix, g_gla_out, g_ffn, g_final = norms
    x2d = x.reshape(g_rows * t, d)
    rows_per_group = t if sh1.shape[1] == 1 else g_rows * t
    q, k, v, k_heads, v_heads, qg, kg, vg, rg, la = _inproj(
        x2d, sh1, sc1, g_mix, w_sb, w_gl, w_gate, b_gate, tm, rows_per_group)
    r3 = lambda a: a.reshape(g_rows, t, a.shape[-1])
    o_sb = sb_fn(r3(q), r3(k), r3(v))
    o_g, st = _gla(r3(qg), r3(kg), r3(vg), r3(rg), r3(la), s0_t, g_gla_out)
    y = _post(x2d, o_sb.reshape(g_rows * t, -1), o_g.reshape(g_rows * t, -1), ga1, sh2, sc2, ga2,
              g_ffn, g_final, w_out, w_up, w_down, tm, rows_per_group)
    return (y.reshape(g_rows, t, d), k_heads.reshape(g_rows, t, SB_HEADS, SB_HEAD_DIM),
            v_heads.reshape(g_rows, t, SB_HEADS, SB_HEAD_DIM), _state_from_t(st))


def kernel(x_prompt, x_sample, c_prompt, c_sample, cache_k, cache_v, state_gla, page_table, w_ada, b_ada, g_mix, w_in, b_sb, w_gla_gate, b_gla_gate, g_gla_out, w_out, g_ffn, w_up, w_down, g_final):
    depth = w_ada.shape[0]
    assert depth == 1, "final-norm fusion assumes a single layer"
    nb, t, d = x_prompt.shape
    nd, td, _ = x_sample.shape
    l = 0

    c_all = jnp.concatenate([c_prompt, c_sample], axis=0)
    pad_rows = (-c_all.shape[0]) % SUBLANES
    mod = _ada(jnp.pad(c_all, ((0, pad_rows), (0, 0))), w_ada[l], b_ada[l])
    mods_p = [m[:nb].reshape(nb, 1, d) for m in jnp.split(mod, 6, axis=-1)]
    mods_s = [jnp.repeat(m[nb:nb + nd], td, axis=0).reshape(1, nd * td, d)
              for m in jnp.split(mod, 6, axis=-1)]

    weights = _prep_weights(w_in[l], w_gla_gate[l], b_gla_gate[l], w_out[l], w_up[l], w_down[l])
    norms = (g_mix[l].reshape(1, d), g_gla_out[l].reshape(1, -1), g_ffn[l].reshape(1, d),
             g_final.reshape(1, d))

    sb_p = lambda q, k, v: _sb_prompt(q, k, v, b_sb[l])
    zeros_t = jnp.zeros((nb, GLA_HEADS // 2, 2 * GLA_VAL_DIM, 2 * GLA_KEY_DIM), F32)
    y_p, k_p, v_p, s_p = _layer(x_prompt, mods_p, weights, norms, sb_p, zeros_t, min(ROW_TILE, t))

    sb_s = lambda q, k, v: _sb_decode(q, k, v, cache_k[l], cache_v[l], page_table, b_sb[l])
    y_s, k_s, v_s, s_s = _layer(x_sample, mods_s, weights, norms, sb_s, _state_to_t(state_gla[l]),
                                nd * td)

    return (y_p, y_s, k_p[None], v_p[None], s_p[None], k_s[None], v_s[None], s_s[None])
```

```python
import functools

import numpy as np
import jax
import jax.numpy as jnp
from jax import lax
from jax.experimental import pallas as pl
from jax.experimental.pallas import tpu as pltpu

F32 = jnp.float32
BF16 = jnp.bfloat16

D_MODEL = 1024
SB_HEADS = 8
SB_HEAD_DIM = 64
SB_WIDTH = SB_HEADS * SB_HEAD_DIM
GLA_HEADS = 4
GLA_KEY_DIM = 64
GLA_VAL_DIM = 128
GLA_QK_WIDTH = GLA_HEADS * GLA_KEY_DIM
GLA_V_WIDTH = GLA_HEADS * GLA_VAL_DIM
GLA_GATE_RANK = 16
GLA_GATE_TEMP = 16.0
GLA_CHUNK = 16
D_FF = 4 * D_MODEL
NORM_EPS = 1e-6

LANES = 128
SUBLANES = 8
VMEM_LIMIT = 56 * 1024 * 1024

SB_BQ = 512
SB_BK = 256
GLA_TB = 128
GLA_STEP_BLOCKS = 4
DEC_PAGES_PER_STEP = 16
ROW_TILE = 512
ADA_TN = 1024
FF_CHUNK = 1024
SB_MASKED = -1e30

_NT = (((1,), (1,)), ((), ()))
_TN = (((0,), (0,)), ((), ()))


LOG2E = float(np.log2(np.e))


def _softplus2(z2):
    return jnp.maximum(z2, 0.0) + jnp.log(1.0 + jnp.exp2(-jnp.abs(z2))) * LOG2E


def _silu(x):
    return x * jax.nn.sigmoid(x)


def _rms(x, gain):
    ms = jnp.mean(x * x, axis=-1, keepdims=True)
    return x * lax.rsqrt(ms + NORM_EPS) * gain


def _ada_body(c_ref, w_ref, b_ref, o_ref):
    s = _silu(c_ref[...])
    o_ref[...] = jnp.dot(s.astype(BF16), w_ref[...].astype(BF16),
                         preferred_element_type=F32) + b_ref[...]


def _ada(c, w_ada, b_ada):
    rows, d = c.shape
    n = w_ada.shape[1]
    return pl.pallas_call(
        _ada_body,
        grid=(n // ADA_TN,),
        in_specs=[pl.BlockSpec((rows, d), lambda j: (0, 0)),
                  pl.BlockSpec((d, ADA_TN), lambda j: (0, j)),
                  pl.BlockSpec((1, ADA_TN), lambda j: (0, j))],
        out_specs=pl.BlockSpec((rows, ADA_TN), lambda j: (0, j)),
        out_shape=jax.ShapeDtypeStruct((rows, n), F32),
        name="ada_mod",
    )(c, w_ada, b_ada.reshape(1, n))


def _rows_to_heads(x):
    rows = x.shape[0]
    assert SB_HEADS == SUBLANES
    groups = rows // SUBLANES
    t = [x[:, h * SB_HEAD_DIM:(h + 1) * SB_HEAD_DIM].reshape(groups, SUBLANES, SB_HEAD_DIM)
         for h in range(SB_HEADS)]
    sub = lax.broadcasted_iota(jnp.int32, t[0].shape, 1)
    for dist in (4, 2, 1):
        keep = (sub & dist) == 0
        y = list(t)
        for v in range(SB_HEADS):
            if v & dist == 0:
                a, b = t[v], t[v + dist]
                y[v] = jnp.where(keep, a, pltpu.roll(b, dist, axis=1))
                y[v + dist] = jnp.where(keep, pltpu.roll(a, SUBLANES - dist, axis=1), b)
        t = y
    return jnp.stack(t, axis=1).reshape(rows, SB_HEADS, SB_HEAD_DIM)


def _inproj_body(x_ref, sh_ref, sc_ref, g_ref, wsb_ref, wgl_ref, wgate_ref, bgate_ref,
                 q_ref, k_ref, v_ref, kh_ref, vh_ref, qg_ref, kg_ref, vg_ref, rg_ref, la_ref):
    h = _rms(x_ref[...], g_ref[...]) * (1.0 + sc_ref[0]) + sh_ref[0]
    hb = h.astype(BF16)
    p1 = jnp.dot(hb, wsb_ref[...], preferred_element_type=F32)
    q_ref[...] = (p1[:, :SB_WIDTH] * (LOG2E * SB_HEAD_DIM ** -0.5)).astype(BF16)
    k = p1[:, SB_WIDTH:2 * SB_WIDTH]
    v = p1[:, 2 * SB_WIDTH:3 * SB_WIDTH]
    k_ref[...] = k.astype(BF16)
    v_ref[...] = v.astype(BF16)
    kh_ref[...] = _rows_to_heads(k)
    vh_ref[...] = _rows_to_heads(v)
    p2 = jnp.dot(hb, wgl_ref[...], preferred_element_type=F32)
    o = 0
    qg_ref[...] = p2[:, o:o + GLA_QK_WIDTH] * (GLA_KEY_DIM ** -0.5)
    o += GLA_QK_WIDTH
    kg_ref[...] = p2[:, o:o + GLA_QK_WIDTH]
    o += GLA_QK_WIDTH
    vg_ref[...] = p2[:, o:o + GLA_V_WIDTH].astype(BF16)
    o += GLA_V_WIDTH
    rg_ref[...] = p2[:, o:o + GLA_V_WIDTH]
    o += GLA_V_WIDTH
    a_low = p2[:, o:o + LANES]
    xg = jnp.dot(a_low.astype(BF16), wgate_ref[...], preferred_element_type=F32) + bgate_ref[...]
    la_ref[...] = (jnp.minimum(xg, 0.0) - jnp.log(1.0 + jnp.exp(-jnp.abs(xg)))) * (1.0 / GLA_GATE_TEMP)


def _mod_spec(mod, tm, rows_per_group):
    if mod.shape[1] == 1:
        tiles_per_group = rows_per_group // tm
        return pl.BlockSpec((1, 1, mod.shape[2]), lambda i: (i // tiles_per_group, 0, 0))
    return pl.BlockSpec((1, tm, mod.shape[2]), lambda i: (0, i, 0))


def _inproj(x2d, sh, sc, g_mix, w_sb, w_gl, w_gate, b_gate, tm, rows_per_group):
    m, d = x2d.shape
    row = lambda w: pl.BlockSpec((tm, w), lambda i: (i, 0))
    full = lambda a: pl.BlockSpec(a.shape, lambda i: (0,) * a.ndim)
    heads = ((SB_HEADS, SB_HEAD_DIM), F32)
    outs = [(SB_WIDTH, BF16), (SB_WIDTH, BF16), (SB_WIDTH, BF16), heads, heads, (GLA_QK_WIDTH, F32),
            (GLA_QK_WIDTH, F32), (GLA_V_WIDTH, BF16), (GLA_V_WIDTH, F32), (GLA_QK_WIDTH, F32)]
    tail = lambda w: w if isinstance(w, tuple) else (w,)
    out_spec = lambda w: pl.BlockSpec((tm,) + tail(w), lambda i: (i,) + (0,) * len(tail(w)))
    return pl.pallas_call(
        _inproj_body,
        grid=(m // tm,),
        in_specs=[row(d), _mod_spec(sh, tm, rows_per_group), _mod_spec(sc, tm, rows_per_group),
                  full(g_mix), full(w_sb), full(w_gl), full(w_gate), full(b_gate)],
        out_specs=[out_spec(w) for w, _ in outs],
        out_shape=[jax.ShapeDtypeStruct((m,) + tail(w), dt) for w, dt in outs],
        compiler_params=pltpu.CompilerParams(dimension_semantics=("parallel",),
                                             vmem_limit_bytes=VMEM_LIMIT),
        name="in_proj",
    )(x2d, sh, sc, g_mix, w_sb, w_gl, w_gate, b_gate)


def _sb_prompt_body(bias_ref, q_ref, k_ref, v_ref, u_ref, o_ref,
                    mask_sc, z_sc, e_sc, carry_sc, acc_sc):
    bq = q_ref.shape[0]
    bk = u_ref.shape[0]
    r = bq // bk
    p = pl.program_id(1)
    i = pl.program_id(2)
    n = r * (i + 1)

    @pl.when(i == 0)
    def _():
        row = lax.broadcasted_iota(jnp.int32, (bq, bk), 0)
        col = lax.broadcasted_iota(jnp.int32, (bq, bk), 1)
        for hh in range(2):
            b = bias_ref[2 * p + hh] * LOG2E
            for t in range(r):
                mask_sc[hh, t] = jnp.where((r - 1 - t) * bk + col < row, b, SB_MASKED)
            mask_sc[hh, r] = jnp.full((bq, bk), b, F32)
            mask_sc[hh, r + 1] = jnp.full((bq, bk), SB_MASKED, F32)

    q = q_ref[...]
    lane = lax.broadcasted_iota(jnp.int32, (bq, LANES), 1)
    first_head = lane < SB_HEAD_DIM
    qh = (jnp.where(first_head, q, jnp.zeros_like(q)), jnp.where(first_head, jnp.zeros_like(q), q))
    dead = jnp.full((bq, bk), SB_MASKED, F32)
    for hh in range(2):
        z_sc[1, hh] = dead
        e_sc[0, hh] = dead.astype(BF16)
        carry_sc[hh] = jnp.zeros((bq, LANES), F32)
    acc_sc[...] = jnp.zeros((bq, LANES), F32)
    first_head_k = lax.broadcasted_iota(jnp.int32, (bk, LANES), 1) < SB_HEAD_DIM

    def tile_start(t):
        j = r * i + r - 1 - jnp.clip(t, 0, n - 1)
        return pl.multiple_of(j * bk, bk)

    def iteration(t, slot):
        kb = k_ref[pl.ds(tile_start(t), bk), :]
        m = jnp.where(t < n, jnp.minimum(t, r), r + 1)
        for hh in range(2):
            z_sc[slot, hh] = mask_sc[hh, m] + lax.dot_general(qh[hh], kb, _NT,
                                                              preferred_element_type=F32)
        for hh in range(2):
            z = z_sc[1 - slot, hh]
            c = jnp.concatenate([carry_sc[hh]] * (bk // LANES), axis=1)
            a = jnp.dot(_softplus2(z).astype(BF16), u_ref[...], preferred_element_type=F32) + c
            e_sc[1 - slot, hh] = (z - a).astype(BF16)
            carry_sc[hh] = jnp.broadcast_to(a[:, 0:1], (bq, LANES))
        vb = v_ref[pl.ds(tile_start(t - 2), bk), :]
        zero = jnp.zeros_like(vb)
        v2 = jnp.concatenate([jnp.where(first_head_k, vb, zero), jnp.where(first_head_k, zero, vb)],
                             axis=0)
        w2 = jnp.concatenate([jnp.exp2(e_sc[slot, 0]), jnp.exp2(e_sc[slot, 1])], axis=1)
        acc_sc[...] += jnp.dot(w2, v2, preferred_element_type=F32)

    def body(tt, _):
        for s in range(4):
            iteration(4 * tt + s, s % 2)
        return 0

    total = n + 2 + (n % 2)
    lax.fori_loop(0, total // 4, body, 0)

    @pl.when(total % 4 != 0)
    def _():
        iteration(total - 2, 0)
        iteration(total - 1, 1)
    o_ref[...] = acc_sc[...].astype(BF16)


def _suffix_ones(n):
    r = np.arange(n)
    return jnp.asarray((r[:, None] >= r[None, :]).astype(np.float32), dtype=BF16)


def _sb_prompt(q, k, v, b_sb):
    b, t, w = q.shape
    bq = min(SB_BQ, t)
    bk = min(SB_BK, t)
    pairs = w // LANES
    return pl.pallas_call(
        _sb_prompt_body,
        grid=(b, pairs, t // bq),
        in_specs=[pl.BlockSpec(memory_space=pltpu.SMEM),
                  pl.BlockSpec((None, bq, LANES), lambda bb, p, i: (bb, i, p)),
                  pl.BlockSpec((None, t, LANES), lambda bb, p, i: (bb, 0, p)),
                  pl.BlockSpec((None, t, LANES), lambda bb, p, i: (bb, 0, p)),
                  pl.BlockSpec((bk, bk), lambda bb, p, i: (0, 0))],
        out_specs=pl.BlockSpec((None, bq, LANES), lambda bb, p, i: (bb, i, p)),
        out_shape=jax.ShapeDtypeStruct((b, t, w), BF16),
        scratch_shapes=[pltpu.VMEM((2, bq // bk + 2, bq, bk), F32),
                        pltpu.VMEM((2, 2, bq, bk), F32),
                        pltpu.VMEM((2, 2, bq, bk), BF16),
                        pltpu.VMEM((2, bq, LANES), F32),
                        pltpu.VMEM((bq, LANES), F32)],
        compiler_params=pltpu.CompilerParams(
            dimension_semantics=("parallel", "parallel", "arbitrary"),
            vmem_limit_bytes=VMEM_LIMIT),
        name="sb_prompt",
    )(b_sb, q, k, v, _suffix_ones(bk))


def _sb_decode_body(pt_ref, bias_ref, q_ref, kn_ref, vn_ref, *rest):
    del pt_ref
    npg = DEC_PAGES_PER_STEP
    k_pages = rest[:npg]
    v_pages = rest[npg:2 * npg]
    u_ref, o_ref, carry_sc, acc_sc = rest[2 * npg:]
    g = pl.program_id(1)
    n_tok, width = q_ref.shape[1:]
    rows = SB_HEADS * n_tok
    page = k_pages[0].shape[2]

    row = lax.broadcasted_iota(jnp.int32, (rows, width), 0)
    lane = lax.broadcasted_iota(jnp.int32, (rows, width), 1)
    own = (row // n_tok) == (lane // SB_HEAD_DIM)
    qbd = jnp.where(own, jnp.concatenate([q_ref[0].astype(F32)] * SB_HEADS, axis=0), 0.0).astype(BF16)
    row_p = lax.broadcasted_iota(jnp.int32, (rows, page), 0)
    col_p = lax.broadcasted_iota(jnp.int32, (rows, page), 1)
    bias = jnp.zeros((rows, page), F32)
    for h in range(SB_HEADS):
        bias = jnp.where(row_p // n_tok == h, bias_ref[h] * LOG2E, bias)
    u = u_ref[...]

    @pl.when(g == 0)
    def _():
        pad = jnp.zeros((page - n_tok, width), F32)
        kb = jnp.concatenate([kn_ref[0].astype(F32), pad], axis=0).astype(BF16)
        vb = jnp.concatenate([vn_ref[0].astype(F32), pad], axis=0).astype(BF16)
        causal = col_p < (row_p % n_tok)
        z = lax.dot_general(qbd, kb, _NT, preferred_element_type=F32) + bias
        sp = jnp.where(causal, _softplus2(z), 0.0)
        a = jnp.dot(sp.astype(BF16), u, preferred_element_type=F32)
        w = jnp.where(causal, jnp.exp2(z - a), 0.0)
        carry_sc[...] = jnp.broadcast_to(a[:, 0:1], carry_sc.shape)
        acc_sc[...] = jnp.dot(w.astype(BF16), vb, preferred_element_type=F32)

    order = list(reversed(range(npg)))
    as_matrix = lambda ref: ref[...].reshape(width, page).astype(BF16)
    zs = [jnp.dot(qbd, as_matrix(k_pages[n]), preferred_element_type=F32) + bias for n in order]
    sp = _softplus2(jnp.concatenate(zs, axis=0))
    a_all = jnp.dot(sp.astype(BF16), u, preferred_element_type=F32)
    c = carry_sc[...]
    acc = acc_sc[...]
    for i, n in enumerate(order):
        a_i = a_all[i * rows:(i + 1) * rows]
        w = jnp.exp2(zs[i] - a_i - c)
        c = c + jnp.broadcast_to(a_i[:, 0:1], c.shape)
        acc = acc + lax.dot_general(w.astype(BF16), as_matrix(v_pages[n]), _NT,
                                    preferred_element_type=F32)
    carry_sc[...] = c
    acc_sc[...] = acc

    @pl.when(g == pl.num_programs(1) - 1)
    def _():
        own_acc = jnp.where(own, acc, 0.0)
        out = own_acc[0:n_tok]
        for h in range(1, SB_HEADS):
            out = out + own_acc[h * n_tok:(h + 1) * n_tok]
        o_ref[0] = out.astype(BF16)


def _sb_decode(q, k_new, v_new, cache_k, cache_v, page_table, b_sb):
    nb, n_tok, w = q.shape
    n_pages = page_table.shape[1]
    page = cache_k.shape[1]
    npg = DEC_PAGES_PER_STEP
    groups = n_pages // npg
    rows = SB_HEADS * n_tok
    cache_k = jnp.transpose(cache_k, (0, 2, 3, 1))
    cache_v = jnp.transpose(cache_v, (0, 2, 3, 1))

    def page_map(n):
        return lambda b, g, pt: (pt[b, (groups - 1 - g) * npg + n], 0, 0, 0)

    tok_spec = pl.BlockSpec((1, n_tok, w), lambda b, g, pt: (b, 0, 0))
    page_specs = [pl.BlockSpec((None, SB_HEADS, SB_HEAD_DIM, page), page_map(n)) for n in range(npg)]
    grid_spec = pltpu.PrefetchScalarGridSpec(
        num_scalar_prefetch=1,
        grid=(nb, groups),
        in_specs=[pl.BlockSpec(memory_space=pltpu.SMEM), tok_spec, tok_spec, tok_spec]
        + page_specs + page_specs
        + [pl.BlockSpec((page, page), lambda b, g, pt: (0, 0))],
        out_specs=tok_spec,
        scratch_shapes=[pltpu.VMEM((rows, LANES), F32), pltpu.VMEM((rows, w), F32)],
    )
    return pl.pallas_call(
        _sb_decode_body,
        grid_spec=grid_spec,
        out_shape=jax.ShapeDtypeStruct((nb, n_tok, w), BF16),
        compiler_params=pltpu.CompilerParams(
            dimension_semantics=("parallel", "arbitrary"), vmem_limit_bytes=VMEM_LIMIT),
        name="sb_decode",
    )(page_table, b_sb, q, k_new, v_new, *([cache_k] * npg), *([cache_v] * npg), _suffix_ones(page))


def _gla_consts(tb, chunk):
    i = np.arange(tb)[:, None]
    t = np.arange(tb)[None, :]
    mats = []
    lvl = np.zeros((tb, tb), np.float32)
    code = 1
    s = tb
    while s > chunk:
        mid = (i // s) * s + s // 2
        upper = i >= mid
        mats.append(np.where(upper & (t >= mid) & (t <= i), 1.0, 0.0)
                    + np.where(~upper & (t > i) & (t < mid), -1.0, 0.0))
        mid_t = (t // s) * s + s // 2
        lvl = np.where((i // s == t // s) & upper & (t < mid_t), code, lvl)
        code += 1
        s //= 2
    mats.append(np.where((t // chunk == i // chunk) & (t <= i), 1.0, 0.0))
    lvl = np.where((i // chunk == t // chunk) & (t <= i), code, lvl)
    mats.append(np.where(t <= i, 1.0, 0.0))
    stack = np.concatenate(mats, axis=0).astype(np.float32)
    return jnp.asarray(stack, dtype=BF16), jnp.asarray(lvl, dtype=F32), code


def _gla_body(q_ref, k_ref, v_ref, r_ref, la_ref, s0_ref, g_ref, m_ref, lvl_ref,
              o_ref, st_ref, s_sc, *, n_levels):
    tb = lvl_ref.shape[0]
    n_tok = min(q_ref.shape[1], tb)
    n_sub = max(q_ref.shape[1] // tb, 1)
    blk = pl.program_id(1)

    @pl.when(blk == 0)
    def _():
        s_sc[...] = s0_ref[0]

    def load(ref, sub, lo, width):
        x = ref[0, sub * n_tok:(sub + 1) * n_tok, lo:lo + width]
        if n_tok < tb:
            x = jnp.concatenate([x.astype(F32), jnp.zeros((tb - n_tok, width), F32)], axis=0).astype(x.dtype)
        return x

    lvl = lvl_ref[...]
    lane = lax.broadcasted_iota(jnp.int32, (tb, LANES), 1)
    first_head = lane < GLA_KEY_DIM
    row_s = lax.broadcasted_iota(jnp.int32, (2 * GLA_VAL_DIM, LANES), 0)
    lane_s = lax.broadcasted_iota(jnp.int32, (2 * GLA_VAL_DIM, LANES), 1)
    own_state = (row_s < GLA_VAL_DIM) == (lane_s < GLA_KEY_DIM)
    gain = g_ref[...]

    blocks = [(p, sub) for p in range(GLA_HEADS // 2) for sub in range(n_sub)]

    def decays(p, sub):
        la = load(la_ref, sub, p * LANES, LANES)
        hi = la.astype(BF16)
        lo = (la - hi.astype(F32)).astype(BF16)
        res = jnp.dot(m_ref[...], jnp.concatenate([hi, lo], axis=1), preferred_element_type=F32)
        dall = res[:, :LANES] + res[:, LANES:]
        return [dall[n * tb:(n + 1) * tb] for n in range(n_levels + 1)]

    def scaled(p, sub, d):
        bc = d[n_levels]
        e_tot = bc[tb - 1:tb, :]
        q = load(q_ref, sub, p * LANES, LANES)
        k = load(k_ref, sub, p * LANES, LANES)
        qs, ks = [], []
        for n in range(n_levels - 1):
            x = jnp.exp(-jnp.abs(d[n]))
            qs.append((q * x, (k * x).astype(BF16)))
        qs.append((q * jnp.exp(d[n_levels - 1]), (k * jnp.exp(-d[n_levels - 1])).astype(BF16)))
        return dict(qk=qs, q_hat=(q * jnp.exp(bc)).astype(BF16),
                    k_hat=(k * jnp.exp(e_tot - bc)).astype(BF16), decay=jnp.exp(e_tot))

    def scores(s):
        out = []
        for hh in range(2):
            head = first_head if hh == 0 else jnp.logical_not(first_head)
            sc = jnp.zeros((tb, tb), F32)
            for n, (qn, kn) in enumerate(s["qk"]):
                s_n = lax.dot_general(jnp.where(head, qn, 0.0).astype(BF16), kn, _NT,
                                      preferred_element_type=F32)
                sc = jnp.where(lvl == float(n + 1), s_n, sc)
            out.append(sc.astype(BF16))
        return out

    def values(p, sub, s, sc):
        o_intra = jnp.concatenate(
            [jnp.dot(sc[hh], load(v_ref, sub, (2 * p + hh) * GLA_VAL_DIM, GLA_VAL_DIM),
                     preferred_element_type=F32) for hh in range(2)], axis=1)
        v_pair = load(v_ref, sub, 2 * p * GLA_VAL_DIM, 2 * GLA_VAL_DIM)
        ds_t = lax.dot_general(v_pair, s["k_hat"], _TN, preferred_element_type=F32)
        return o_intra, jnp.where(own_state, ds_t, 0.0)

    ds = [decays(p, sub) for p, sub in blocks]
    ss = [scaled(p, sub, d) for (p, sub), d in zip(blocks, ds)]
    scs = [scores(s) for s in ss]
    vals = [values(p, sub, s, sc) for (p, sub), s, sc in zip(blocks, ss, scs)]

    st = None
    for (p, sub), s, (o_intra, ds_t) in zip(blocks, ss, vals):
        if sub == 0:
            st = s_sc[p]
        o = o_intra + lax.dot_general(s["q_hat"], st.astype(BF16), _NT, preferred_element_type=F32)
        rows = slice(sub * n_tok, (sub + 1) * n_tok)
        for hh in range(2):
            cols = slice((2 * p + hh) * GLA_VAL_DIM, (2 * p + hh + 1) * GLA_VAL_DIM)
            oh = _rms(o[:n_tok, hh * GLA_VAL_DIM:(hh + 1) * GLA_VAL_DIM], gain)
            o_ref[0, rows, cols] = (oh * _silu(r_ref[0, rows, cols])).astype(BF16)
        st = st * s["decay"] + ds_t
        if sub == n_sub - 1:
            s_sc[p] = st
            st_ref[0, p] = st


def _gla(qg, kg, vg, rg, la, s0_t, g_out):
    b, t, _ = qg.shape
    tb = GLA_TB
    tok = min(t, tb * GLA_STEP_BLOCKS)
    nblk = t // tok
    m_stack, lvl, n_levels = _gla_consts(tb, GLA_CHUNK if t % GLA_CHUNK == 0 else t)
    blk = lambda w: pl.BlockSpec((1, tok, w), lambda bb, j: (bb, j, 0))
    full = lambda a: pl.BlockSpec(a.shape, lambda bb, j: (0,) * a.ndim)
    st_spec = pl.BlockSpec((1,) + s0_t.shape[1:], lambda bb, j: (bb, 0, 0, 0))
    return pl.pallas_call(
        functools.partial(_gla_body, n_levels=n_levels),
        grid=(b, nblk),
        in_specs=[blk(GLA_QK_WIDTH), blk(GLA_QK_WIDTH), blk(GLA_V_WIDTH), blk(GLA_V_WIDTH),
                  blk(GLA_QK_WIDTH), st_spec, full(g_out), full(m_stack), full(lvl)],
        out_specs=[blk(GLA_V_WIDTH), st_spec],
        out_shape=[jax.ShapeDtypeStruct((b, t, GLA_V_WIDTH), BF16),
                   jax.ShapeDtypeStruct(s0_t.shape, F32)],
        scratch_shapes=[pltpu.VMEM(s0_t.shape[1:], F32)],
        compiler_params=pltpu.CompilerParams(
            dimension_semantics=("parallel", "arbitrary"), vmem_limit_bytes=VMEM_LIMIT),
        name="gla",
    )(qg, kg, vg, rg, la, s0_t, g_out, m_stack, lvl)


def _state_to_t(s):
    b = s.shape[0]
    st = jnp.swapaxes(s, 2, 3).reshape(b, GLA_HEADS // 2, 2, GLA_VAL_DIM, GLA_KEY_DIM)
    z = jnp.zeros_like(st[:, :, 0])
    top = jnp.concatenate([st[:, :, 0], z], axis=-1)
    bot = jnp.concatenate([z, st[:, :, 1]], axis=-1)
    return jnp.concatenate([top, bot], axis=2)


def _state_from_t(st):
    b = st.shape[0]
    h0 = st[:, :, :GLA_VAL_DIM, :GLA_KEY_DIM]
    h1 = st[:, :, GLA_VAL_DIM:, GLA_KEY_DIM:]
    s = jnp.stack([h0, h1], axis=2).reshape(b, GLA_HEADS, GLA_VAL_DIM, GLA_KEY_DIM)
    return jnp.swapaxes(s, 2, 3)


def _post_body(x_ref, osb_ref, og_ref, ga1_ref, sh2_ref, sc2_ref, ga2_ref, gffn_ref, gfin_ref,
               wo_ref, wup_ref, wdn_ref, y_ref):
    mix = (jnp.dot(osb_ref[...], wo_ref[:SB_WIDTH, :], preferred_element_type=F32)
           + jnp.dot(og_ref[...], wo_ref[SB_WIDTH:, :], preferred_element_type=F32))
    x1 = x_ref[...] + ga1_ref[0] * mix
    h2 = (_rms(x1, gffn_ref[...]) * (1.0 + sc2_ref[0]) + sh2_ref[0]).astype(BF16)
    ff = jnp.zeros_like(x1)
    for c in range(D_FF // FF_CHUNK):
        u = jnp.dot(h2, wup_ref[:, c * FF_CHUNK:(c + 1) * FF_CHUNK], preferred_element_type=F32)
        a = jnp.square(jnp.maximum(u, 0.0)).astype(BF16)
        ff = ff + jnp.dot(a, wdn_ref[c * FF_CHUNK:(c + 1) * FF_CHUNK, :], preferred_element_type=F32)
    x2 = x1 + ga2_ref[0] * ff
    y_ref[...] = _rms(x2, gfin_ref[...])


def _post(x2d, o_sb, o_g, ga1, sh2, sc2, ga2, g_ffn, g_final, w_out, w_up, w_down, tm, rows_per_group):
    m, d = x2d.shape
    row = lambda w: pl.BlockSpec((tm, w), lambda i: (i, 0))
    full = lambda a: pl.BlockSpec(a.shape, lambda i: (0,) * a.ndim)
    resident = lambda a: pl.BlockSpec(a.shape, lambda i: (0,) * a.ndim, pipeline_mode=pl.Buffered(1))
    mod = lambda a: _mod_spec(a, tm, rows_per_group)
    return pl.pallas_call(
        _post_body,
        grid=(m // tm,),
        in_specs=[row(d), row(SB_WIDTH), row(GLA_V_WIDTH), mod(ga1), mod(sh2), mod(sc2), mod(ga2),
                  full(g_ffn), full(g_final), resident(w_out), resident(w_up), resident(w_down)],
        out_specs=row(d),
        out_shape=jax.ShapeDtypeStruct((m, d), F32),
        compiler_params=pltpu.CompilerParams(dimension_semantics=("parallel",),
                                             vmem_limit_bytes=VMEM_LIMIT),
        name="post_ffn",
    )(x2d, o_sb, o_g, ga1, sh2, sc2, ga2, g_ffn, g_final, w_out, w_up, w_down)


def _prep_weights(w_in, w_gla_gate, b_gla_gate, w_out, w_up, w_down):
    n_sb = 3 * SB_WIDTH
    n_gl = 2 * GLA_QK_WIDTH + 2 * GLA_V_WIDTH
    w_sb = w_in[:, :n_sb].astype(BF16)
    a_pad = jnp.pad(w_in[:, n_sb + n_gl:], ((0, 0), (0, LANES - GLA_GATE_RANK)))
    w_gl = jnp.concatenate([w_in[:, n_sb:n_sb + n_gl], a_pad], axis=1).astype(BF16)
    w_gate = jnp.pad(w_gla_gate, ((0, LANES - GLA_GATE_RANK), (0, 0))).astype(BF16)
    return (w_sb, w_gl, w_gate, b_gla_gate.reshape(1, -1),
            w_out.astype(BF16), w_up.astype(BF16), w_down.astype(BF16))


def _layer(x, mods, weights, norms, sb_fn, s0_t, tm):
    g_rows, t, d = x.shape
    sh1, sc1, ga1, sh2, sc2, ga2 = mods
    w_sb, w_gl, w_gate, b_gate, w_out, w_up, w_down = weights
    g_mix, g_gla_out, g_ffn, g_final = norms
    x2d = x.reshape(g_rows * t, d)
    rows_per_group = t if sh1.shape[1] == 1 else g_rows * t
    q, k, v, k_heads, v_heads, qg, kg, vg, rg, la = _inproj(
        x2d, sh1, sc1, g_mix, w_sb, w_gl, w_gate, b_gate, tm, rows_per_group)
    r3 = lambda a: a.reshape(g_rows, t, a.shape[-1])
    o_sb = sb_fn(r3(q), r3(k), r3(v))
    o_g, st = _gla(r3(qg), r3(kg), r3(vg), r3(rg), r3(la), s0_t, g_gla_out)
    y = _post(x2d, o_sb.reshape(g_rows * t, -1), o_g.reshape(g_rows * t, -1), ga1, sh2, sc2, ga2,
              g_ffn, g_final, w_out, w_up, w_down, tm, rows_per_group)
    return (y.reshape(g_rows, t, d), k_heads.reshape(g_rows, t, SB_HEADS, SB_HEAD_DIM),
            v_heads.reshape(g_rows, t, SB_HEADS, SB_HEAD_DIM), _state_from_t(st))


def kernel(x_prompt, x_sample, c_prompt, c_sample, cache_k, cache_v, state_gla, page_table, w_ada, b_ada, g_mix, w_in, b_sb, w_gla_gate, b_gla_gate, g_gla_out, w_out, g_ffn, w_up, w_down, g_final):
    depth = w_ada.shape[0]
    assert depth == 1, "final-norm fusion assumes a single layer"
    nb, t, d = x_prompt.shape
    nd, td, _ = x_sample.shape
    l = 0

    c_all = jnp.concatenate([c_prompt, c_sample], axis=0)
    pad_rows = (-c_all.shape[0]) % SUBLANES
    mod = _ada(jnp.pad(c_all, ((0, pad_rows), (0, 0))), w_ada[l], b_ada[l])
    mods_p = [m[:nb].reshape(nb, 1, d) for m in jnp.split(mod, 6, axis=-1)]
    mods_s = [jnp.repeat(m[nb:nb + nd], td, axis=0).reshape(1, nd * td, d)
              for m in jnp.split(mod, 6, axis=-1)]

    weights = _prep_weights(w_in[l], w_gla_gate[l], b_gla_gate[l], w_out[l], w_up[l], w_down[l])
    norms = (g_mix[l].reshape(1, d), g_gla_out[l].reshape(1, -1), g_ffn[l].reshape(1, d),
             g_final.reshape(1, d))

    sb_p = lambda q, k, v: _sb_prompt(q, k, v, b_sb[l])
    zeros_t = jnp.zeros((nb, GLA_HEADS // 2, 2 * GLA_VAL_DIM, 2 * GLA_KEY_DIM), F32)
    y_p, k_p, v_p, s_p = _layer(x_prompt, mods_p, weights, norms, sb_p, zeros_t, min(ROW_TILE, t))

    sb_s = lambda q, k, v: _sb_decode(q, k, v, cache_k[l], cache_v[l], page_table, b_sb[l])
    y_s, k_s, v_s, s_s = _layer(x_sample, mods_s, weights, norms, sb_s, _state_to_t(state_gla[l]),
                                nd * td)

    return (y_p, y_s, k_p[None], v_p[None], s_p[None], k_s[None], v_s[None], s_s[None])
```

```python
import functools

import numpy as np
import jax
import jax.numpy as jnp
from jax import lax
from jax.experimental import pallas as pl
from jax.experimental.pallas import tpu as pltpu

F32 = jnp.float32
BF16 = jnp.bfloat16

D_MODEL = 1024
SB_HEADS = 8
SB_HEAD_DIM = 64
SB_WIDTH = SB_HEADS * SB_HEAD_DIM
GLA_HEADS = 4
GLA_KEY_DIM = 64
GLA_VAL_DIM = 128
GLA_QK_WIDTH = GLA_HEADS * GLA_KEY_DIM
GLA_V_WIDTH = GLA_HEADS * GLA_VAL_DIM
GLA_GATE_RANK = 16
GLA_GATE_TEMP = 16.0
GLA_CHUNK = 16
D_FF = 4 * D_MODEL
NORM_EPS = 1e-6

LANES = 128
SUBLANES = 8
VMEM_LIMIT = 56 * 1024 * 1024

SB_BQ = 512
SB_BK = 256
GLA_TB = 128
GLA_STEP_BLOCKS = 4
DEC_PAGES_PER_STEP = 32
ROW_TILE = 1024
ADA_TN = 1024
FF_CHUNK = 1024
SB_MASKED = -1e30

_NT = (((1,), (1,)), ((), ()))
_TN = (((0,), (0,)), ((), ()))


LOG2E = float(np.log2(np.e))


EXP2_MAX = 126.0


def _softplus2(z2):
    return jnp.maximum(z2, jnp.log(1.0 + jnp.exp2(jnp.minimum(z2, EXP2_MAX))) * LOG2E)


def _silu(x):
    return x * jax.nn.sigmoid(x)


def _rms(x, gain):
    ms = jnp.mean(x * x, axis=-1, keepdims=True)
    return x * lax.rsqrt(ms + NORM_EPS) * gain


def _ada_body(c_ref, w_ref, b_ref, o_ref):
    s = _silu(c_ref[...])
    o_ref[...] = jnp.dot(s.astype(BF16), w_ref[...].astype(BF16),
                         preferred_element_type=F32) + b_ref[...]


def _ada(c, w_ada, b_ada):
    rows, d = c.shape
    n = w_ada.shape[1]
    return pl.pallas_call(
        _ada_body,
        grid=(n // ADA_TN,),
        in_specs=[pl.BlockSpec((rows, d), lambda j: (0, 0)),
                  pl.BlockSpec((d, ADA_TN), lambda j: (0, j)),
                  pl.BlockSpec((1, ADA_TN), lambda j: (0, j))],
        out_specs=pl.BlockSpec((rows, ADA_TN), lambda j: (0, j)),
        out_shape=jax.ShapeDtypeStruct((rows, n), F32),
        name="ada_mod",
    )(c, w_ada, b_ada.reshape(1, n))


def _rows_to_heads(x):
    rows = x.shape[0]
    assert SB_HEADS == SUBLANES
    groups = rows // SUBLANES
    t = [x[:, h * SB_HEAD_DIM:(h + 1) * SB_HEAD_DIM].reshape(groups, SUBLANES, SB_HEAD_DIM)
         for h in range(SB_HEADS)]
    sub = lax.broadcasted_iota(jnp.int32, t[0].shape, 1)
    for dist in (4, 2, 1):
        keep = (sub & dist) == 0
        y = list(t)
        for v in range(SB_HEADS):
            if v & dist == 0:
                a, b = t[v], t[v + dist]
                y[v] = jnp.where(keep, a, pltpu.roll(b, dist, axis=1))
                y[v + dist] = jnp.where(keep, pltpu.roll(a, SUBLANES - dist, axis=1), b)
        t = y
    return jnp.stack(t, axis=1).reshape(rows, SB_HEADS, SB_HEAD_DIM)


def _inproj_body(x_ref, sh_ref, sc_ref, g_ref, wsb_ref, wgl_ref, wgate_ref, bgate_ref,
                 q_ref, k_ref, v_ref, kh_ref, vh_ref, qg_ref, kg_ref, vg_ref, rg_ref, la_ref):
    h = _rms(x_ref[...], g_ref[...]) * (1.0 + sc_ref[0]) + sh_ref[0]
    hb = h.astype(BF16)
    p1 = jnp.dot(hb, wsb_ref[...], preferred_element_type=F32)
    q_ref[...] = (p1[:, :SB_WIDTH] * (LOG2E * SB_HEAD_DIM ** -0.5)).astype(BF16)
    k = p1[:, SB_WIDTH:2 * SB_WIDTH]
    v = p1[:, 2 * SB_WIDTH:3 * SB_WIDTH]
    k_ref[...] = k.astype(BF16)
    v_ref[...] = v.astype(BF16)
    kh_ref[...] = _rows_to_heads(k)
    vh_ref[...] = _rows_to_heads(v)
    p2 = jnp.dot(hb, wgl_ref[...], preferred_element_type=F32)
    o = 0
    qg_ref[...] = p2[:, o:o + GLA_QK_WIDTH] * (GLA_KEY_DIM ** -0.5)
    o += GLA_QK_WIDTH
    kg_ref[...] = p2[:, o:o + GLA_QK_WIDTH]
    o += GLA_QK_WIDTH
    vg_ref[...] = p2[:, o:o + GLA_V_WIDTH].astype(BF16)
    o += GLA_V_WIDTH
    rg_ref[...] = p2[:, o:o + GLA_V_WIDTH]
    o += GLA_V_WIDTH
    a_low = p2[:, o:o + LANES]
    xg = jnp.dot(a_low.astype(BF16), wgate_ref[...], preferred_element_type=F32) + bgate_ref[...]
    la_ref[...] = (jnp.minimum(xg, 0.0) - jnp.log(1.0 + jnp.exp(-jnp.abs(xg)))) * (LOG2E / GLA_GATE_TEMP)


def _mod_spec(mod, tm, rows_per_group):
    if mod.shape[1] == 1:
        tiles_per_group = rows_per_group // tm
        return pl.BlockSpec((1, 1, mod.shape[2]), lambda i: (i // tiles_per_group, 0, 0))
    return pl.BlockSpec((1, tm, mod.shape[2]), lambda i: (0, i, 0))


def _inproj(x2d, sh, sc, g_mix, w_sb, w_gl, w_gate, b_gate, tm, rows_per_group):
    m, d = x2d.shape
    row = lambda w: pl.BlockSpec((tm, w), lambda i: (i, 0))
    full = lambda a: pl.BlockSpec(a.shape, lambda i: (0,) * a.ndim)
    heads = ((SB_HEADS, SB_HEAD_DIM), F32)
    outs = [(SB_WIDTH, BF16), (SB_WIDTH, BF16), (SB_WIDTH, BF16), heads, heads, (GLA_QK_WIDTH, F32),
            (GLA_QK_WIDTH, F32), (GLA_V_WIDTH, BF16), (GLA_V_WIDTH, F32), (GLA_QK_WIDTH, F32)]
    tail = lambda w: w if isinstance(w, tuple) else (w,)
    out_spec = lambda w: pl.BlockSpec((tm,) + tail(w), lambda i: (i,) + (0,) * len(tail(w)))
    return pl.pallas_call(
        _inproj_body,
        grid=(m // tm,),
        in_specs=[row(d), _mod_spec(sh, tm, rows_per_group), _mod_spec(sc, tm, rows_per_group),
                  full(g_mix), full(w_sb), full(w_gl), full(w_gate), full(b_gate)],
        out_specs=[out_spec(w) for w, _ in outs],
        out_shape=[jax.ShapeDtypeStruct((m,) + tail(w), dt) for w, dt in outs],
        compiler_params=pltpu.CompilerParams(dimension_semantics=("parallel",),
                                             vmem_limit_bytes=VMEM_LIMIT),
        name="in_proj",
    )(x2d, sh, sc, g_mix, w_sb, w_gl, w_gate, b_gate)


def _sb_prompt_body(bias_ref, q_ref, k_ref, v_ref, u_ref, o_ref,
                    mask_sc, z_sc, e_sc, carry_sc, acc_sc):
    bq = q_ref.shape[0]
    bk = u_ref.shape[0]
    r = bq // bk
    assert r >= 2 and r % 2 == 0, "the hand-over needs an even number (>= 2) of key tiles per query tile"
    p = pl.program_id(1)
    i = pl.program_id(2)
    live = i < pl.num_programs(2) - 1
    n = r * (i + 1)
    last_tile = k_ref.shape[0] // bk - 1

    @pl.when(i == 0)
    def _():
        row = lax.broadcasted_iota(jnp.int32, (bq, bk), 0)
        col = lax.broadcasted_iota(jnp.int32, (bq, bk), 1)
        dead = jnp.full((bq, bk), SB_MASKED, F32)
        for hh in range(2):
            b = bias_ref[2 * p + hh] * LOG2E
            for t in range(r):
                mask_sc[hh, t] = jnp.where((r - 1 - t) * bk + col < row, b, SB_MASKED)
            mask_sc[hh, r] = jnp.full((bq, bk), b, F32)
            mask_sc[hh, r + 1] = dead
            z_sc[1, hh] = dead
            e_sc[0, hh] = dead.astype(BF16)
            carry_sc[hh] = jnp.zeros((bq, LANES), F32)
        acc_sc[...] = jnp.zeros((bq, LANES), F32)

    q = q_ref[...]
    lane = lax.broadcasted_iota(jnp.int32, (bq, LANES), 1)
    first_head = lane < SB_HEAD_DIM
    qh = (jnp.where(first_head, q, jnp.zeros_like(q)), jnp.where(first_head, jnp.zeros_like(q), q))
    first_head_k = lax.broadcasted_iota(jnp.int32, (bk, LANES), 1) < SB_HEAD_DIM

    def tile_start(j):
        return pl.multiple_of(j * bk, bk)

    def iteration(t, slot):
        kb = k_ref[pl.ds(tile_start(jnp.minimum(r * i + r - 1 - t, last_tile)), bk), :]
        m = jnp.where(live, jnp.minimum(t, r), r + 1)
        for hh in range(2):
            z_sc[slot, hh] = mask_sc[hh, m] + lax.dot_general(qh[hh], kb, _NT,
                                                              preferred_element_type=F32)
        for hh in range(2):
            z = z_sc[1 - slot, hh]
            c = jnp.concatenate([carry_sc[hh]] * (bk // LANES), axis=1)
            sp = _softplus2(z)
            later = jnp.dot(sp.astype(BF16), u_ref[...], preferred_element_type=F32) + c
            e_sc[1 - slot, hh] = ((z - sp) - later).astype(BF16)
            carry_sc[hh] = jnp.broadcast_to(later[:, 0:1] + sp[:, 0:1], (bq, LANES))
        jv = jnp.where(t < 2, 1 - t, r * i + r + 1 - t)
        vb = v_ref[pl.ds(tile_start(jv), bk), :]
        zero = jnp.zeros_like(vb)
        v2 = jnp.concatenate([jnp.where(first_head_k, vb, zero), jnp.where(first_head_k, zero, vb)],
                             axis=0)
        w2 = jnp.concatenate([jnp.exp2(e_sc[slot, 0]), jnp.exp2(e_sc[slot, 1])], axis=1)
        acc_sc[...] += jnp.dot(w2, v2, preferred_element_type=F32)

    iteration(0, 0)
    carry_sc[...] = jnp.zeros(carry_sc.shape, F32)
    iteration(1, 1)

    @pl.when(i > 0)
    def _():
        o_ref[...] = acc_sc[...].astype(BF16)
    acc_sc[...] = jnp.zeros((bq, LANES), F32)

    def body(tt, _):
        for s in range(4):
            iteration(2 + 4 * tt + s, s % 2)
        return 0

    rest = jnp.where(live, n - 2, 0)
    lax.fori_loop(0, rest // 4, body, 0)

    @pl.when(rest % 4 != 0)
    def _():
        iteration(rest, 0)
        iteration(rest + 1, 1)


def _later_ones(n):
    r = np.arange(n)
    return jnp.asarray((r[:, None] > r[None, :]).astype(np.float32), dtype=BF16)


def _sb_prompt(q, k, v, b_sb):
    b, t, w = q.shape
    bq = min(SB_BQ, t)
    bk = min(SB_BK, t)
    pairs = w // LANES
    nq = t // bq
    return pl.pallas_call(
        _sb_prompt_body,
        grid=(b, pairs, nq + 1),
        in_specs=[pl.BlockSpec(memory_space=pltpu.SMEM),
                  pl.BlockSpec((None, bq, LANES), lambda bb, p, i: (bb, jnp.minimum(i, nq - 1), p)),
                  pl.BlockSpec((None, t, LANES), lambda bb, p, i: (bb, 0, p)),
                  pl.BlockSpec((None, t, LANES), lambda bb, p, i: (bb, 0, p)),
                  pl.BlockSpec((bk, bk), lambda bb, p, i: (0, 0))],
        out_specs=pl.BlockSpec((None, bq, LANES), lambda bb, p, i: (bb, jnp.maximum(i - 1, 0), p)),
        out_shape=jax.ShapeDtypeStruct((b, t, w), BF16),
        scratch_shapes=[pltpu.VMEM((2, bq // bk + 2, bq, bk), F32),
                        pltpu.VMEM((2, 2, bq, bk), F32),
                        pltpu.VMEM((2, 2, bq, bk), BF16),
                        pltpu.VMEM((2, bq, LANES), F32),
                        pltpu.VMEM((bq, LANES), F32)],
        compiler_params=pltpu.CompilerParams(
            dimension_semantics=("parallel", "parallel", "arbitrary"),
            vmem_limit_bytes=VMEM_LIMIT),
        name="sb_prompt",
    )(b_sb, q, k, v, _later_ones(bk))


def _sb_decode_body(pt_ref, bias_ref, q_ref, kn_ref, vn_ref, *rest):
    del pt_ref
    npg = DEC_PAGES_PER_STEP
    k_pages = rest[:npg]
    v_pages = rest[npg:2 * npg]
    u_ref, o_ref, carry_sc, acc_sc = rest[2 * npg:]
    g = pl.program_id(1)
    n_tok, width = q_ref.shape[1:]
    rows = SB_HEADS * n_tok
    page = k_pages[0].shape[2]

    row = lax.broadcasted_iota(jnp.int32, (rows, width), 0)
    lane = lax.broadcasted_iota(jnp.int32, (rows, width), 1)
    own = (row // n_tok) == (lane // SB_HEAD_DIM)
    qbd = jnp.where(own, jnp.concatenate([q_ref[0].astype(F32)] * SB_HEADS, axis=0), 0.0).astype(BF16)
    row_p = lax.broadcasted_iota(jnp.int32, (rows, page), 0)
    col_p = lax.broadcasted_iota(jnp.int32, (rows, page), 1)
    bias = jnp.zeros((rows, page), F32)
    for h in range(SB_HEADS):
        bias = jnp.where(row_p // n_tok == h, bias_ref[h] * LOG2E, bias)
    u = u_ref[...]

    @pl.when(g == 0)
    def _():
        pad = jnp.zeros((page - n_tok, width), F32)
        kb = jnp.concatenate([kn_ref[0].astype(F32), pad], axis=0).astype(BF16)
        vb = jnp.concatenate([vn_ref[0].astype(F32), pad], axis=0).astype(BF16)
        causal = col_p < (row_p % n_tok)
        z = lax.dot_general(qbd, kb, _NT, preferred_element_type=F32) + bias
        sp = jnp.where(causal, _softplus2(z), 0.0)
        later = jnp.dot(sp.astype(BF16), u, preferred_element_type=F32)
        w = jnp.where(causal, jnp.exp2((z - sp) - later), 0.0)
        carry_sc[...] = jnp.broadcast_to(later[:, 0:1] + sp[:, 0:1], carry_sc.shape)
        acc_sc[...] = jnp.dot(w.astype(BF16), vb, preferred_element_type=F32)

    order = list(reversed(range(npg)))
    as_matrix = lambda ref: ref[...].reshape(width, page).astype(BF16)
    zs = [jnp.dot(qbd, as_matrix(k_pages[n]), preferred_element_type=F32) + bias for n in order]
    z_all = jnp.concatenate(zs, axis=0)
    sp = _softplus2(z_all)
    log_beta = z_all - sp
    later_all = jnp.dot(sp.astype(BF16), u, preferred_element_type=F32)
    c = carry_sc[...]
    acc = acc_sc[...]
    for i, n in enumerate(order):
        pg = slice(i * rows, (i + 1) * rows)
        w = jnp.exp2(log_beta[pg] - later_all[pg] - c)
        c = c + jnp.broadcast_to(later_all[pg][:, 0:1] + sp[pg][:, 0:1], c.shape)
        acc = acc + lax.dot_general(w.astype(BF16), as_matrix(v_pages[n]), _NT,
                                    preferred_element_type=F32)
    carry_sc[...] = c
    acc_sc[...] = acc

    @pl.when(g == pl.num_programs(1) - 1)
    def _():
        own_acc = jnp.where(own, acc, 0.0)
        out = own_acc[0:n_tok]
        for h in range(1, SB_HEADS):
            out = out + own_acc[h * n_tok:(h + 1) * n_tok]
        o_ref[0] = out.astype(BF16)


def _sb_decode(q, k_new, v_new, cache_k, cache_v, page_table, b_sb):
    nb, n_tok, w = q.shape
    n_pages = page_table.shape[1]
    page = cache_k.shape[1]
    npg = DEC_PAGES_PER_STEP
    groups = n_pages // npg
    rows = SB_HEADS * n_tok
    cache_k = jnp.transpose(cache_k, (0, 2, 3, 1))
    cache_v = jnp.transpose(cache_v, (0, 2, 3, 1))

    def page_map(n):
        return lambda b, g, pt: (pt[b, (groups - 1 - g) * npg + n], 0, 0, 0)

    tok_spec = pl.BlockSpec((1, n_tok, w), lambda b, g, pt: (b, 0, 0))
    page_specs = [pl.BlockSpec((None, SB_HEADS, SB_HEAD_DIM, page), page_map(n)) for n in range(npg)]
    grid_spec = pltpu.PrefetchScalarGridSpec(
        num_scalar_prefetch=1,
        grid=(nb, groups),
        in_specs=[pl.BlockSpec(memory_space=pltpu.SMEM), tok_spec, tok_spec, tok_spec]
        + page_specs + page_specs
        + [pl.BlockSpec((page, page), lambda b, g, pt: (0, 0))],
        out_specs=tok_spec,
        scratch_shapes=[pltpu.VMEM((rows, LANES), F32), pltpu.VMEM((rows, w), F32)],
    )
    return pl.pallas_call(
        _sb_decode_body,
        grid_spec=grid_spec,
        out_shape=jax.ShapeDtypeStruct((nb, n_tok, w), BF16),
        compiler_params=pltpu.CompilerParams(
            dimension_semantics=("parallel", "arbitrary"), vmem_limit_bytes=VMEM_LIMIT),
        name="sb_decode",
    )(page_table, b_sb, q, k_new, v_new, *([cache_k] * npg), *([cache_v] * npg), _later_ones(page))


def _gla_consts(tb, chunk):
    i = np.arange(tb)[:, None]
    t = np.arange(tb)[None, :]
    mats = []
    lvl = np.zeros((tb, tb), np.float32)
    code = 1
    s = tb
    while s > chunk:
        mid = (i // s) * s + s // 2
        upper = i >= mid
        mats.append(np.where(upper & (t >= mid) & (t <= i), 1.0, 0.0)
                    + np.where(~upper & (t > i) & (t < mid), -1.0, 0.0))
        mid_t = (t // s) * s + s // 2
        lvl = np.where((i // s == t // s) & upper & (t < mid_t), code, lvl)
        code += 1
        s //= 2
    mats.append(np.where((t // chunk == i // chunk) & (t <= i), 1.0, 0.0))
    lvl = np.where((i // chunk == t // chunk) & (t <= i), code, lvl)
    mats.append(np.where(t <= i, 1.0, 0.0))
    stack = np.concatenate(mats, axis=0).astype(np.float32)
    return jnp.asarray(stack, dtype=BF16), jnp.asarray(lvl, dtype=F32), code


def _gla_body(q_ref, k_ref, v_ref, r_ref, la_ref, s0_ref, g_ref, m_ref, lvl_ref,
              o_ref, st_ref, s_sc, *, n_levels):
    tb = lvl_ref.shape[0]
    n_tok = min(q_ref.shape[1], tb)
    n_sub = max(q_ref.shape[1] // tb, 1)
    blk = pl.program_id(1)

    @pl.when(blk == 0)
    def _():
        s_sc[...] = s0_ref[0]

    def load(ref, sub, lo, width):
        x = ref[0, sub * n_tok:(sub + 1) * n_tok, lo:lo + width]
        if n_tok < tb:
            x = jnp.concatenate([x.astype(F32), jnp.zeros((tb - n_tok, width), F32)], axis=0).astype(x.dtype)
        return x

    at_level = [lvl_ref[...] == float(n + 1) for n in range(n_levels)]
    lane = lax.broadcasted_iota(jnp.int32, (tb, LANES), 1)
    first_head = lane < GLA_KEY_DIM
    row_s = lax.broadcasted_iota(jnp.int32, (2 * GLA_VAL_DIM, LANES), 0)
    lane_s = lax.broadcasted_iota(jnp.int32, (2 * GLA_VAL_DIM, LANES), 1)
    own_state = (row_s < GLA_VAL_DIM) == (lane_s < GLA_KEY_DIM)
    gain = g_ref[...]

    blocks = [(p, sub) for p in range(GLA_HEADS // 2) for sub in range(n_sub)]

    def decays(p, sub):
        la = load(la_ref, sub, p * LANES, LANES)
        hi = la.astype(BF16)
        lo = (la - hi.astype(F32)).astype(BF16)
        res = jnp.dot(m_ref[...], jnp.concatenate([hi, lo], axis=1), preferred_element_type=F32)
        dall = res[:, :LANES] + res[:, LANES:]
        return [dall[n * tb:(n + 1) * tb] for n in range(n_levels + 1)]

    def scaled(p, sub, d):
        bc = d[n_levels]
        e_tot = bc[tb - 1:tb, :]
        q = load(q_ref, sub, p * LANES, LANES)
        k = load(k_ref, sub, p * LANES, LANES)
        qs = []
        for n in range(n_levels - 1):
            x = jnp.exp2(-jnp.abs(d[n]))
            qs.append(((q * x).astype(BF16), (k * x).astype(BF16)))
        qs.append(((q * jnp.exp2(d[n_levels - 1])).astype(BF16),
                   (k * jnp.exp2(-d[n_levels - 1])).astype(BF16)))
        return dict(qk=qs, q_hat=(q * jnp.exp2(bc)).astype(BF16),
                    k_hat=(k * jnp.exp2(e_tot - bc)).astype(BF16), decay=jnp.exp2(e_tot))

    def scores(s):
        out = []
        for hh in range(2):
            head = first_head if hh == 0 else jnp.logical_not(first_head)
            sc = jnp.zeros((tb, tb), F32)
            for n, (qn, kn) in enumerate(s["qk"]):
                s_n = lax.dot_general(jnp.where(head, qn, jnp.zeros_like(qn)), kn, _NT,
                                      preferred_element_type=F32)
                sc = jnp.where(at_level[n], s_n, sc)
            out.append(sc.astype(BF16))
        return out

    def values(p, sub, s, sc):
        o_intra = jnp.concatenate(
            [jnp.dot(sc[hh], load(v_ref, sub, (2 * p + hh) * GLA_VAL_DIM, GLA_VAL_DIM),
                     preferred_element_type=F32) for hh in range(2)], axis=1)
        v_pair = load(v_ref, sub, 2 * p * GLA_VAL_DIM, 2 * GLA_VAL_DIM)
        ds_t = lax.dot_general(v_pair, s["k_hat"], _TN, preferred_element_type=F32)
        return o_intra, jnp.where(own_state, ds_t, 0.0)

    ds = [decays(p, sub) for p, sub in blocks]
    ss = [scaled(p, sub, d) for (p, sub), d in zip(blocks, ds)]
    scs = [scores(s) for s in ss]
    vals = [values(p, sub, s, sc) for (p, sub), s, sc in zip(blocks, ss, scs)]

    st = None
    for (p, sub), s, (o_intra, ds_t) in zip(blocks, ss, vals):
        if sub == 0:
            st = s_sc[p]
        o = o_intra + lax.dot_general(s["q_hat"], st.astype(BF16), _NT, preferred_element_type=F32)
        rows = slice(sub * n_tok, (sub + 1) * n_tok)
        for hh in range(2):
            cols = slice((2 * p + hh) * GLA_VAL_DIM, (2 * p + hh + 1) * GLA_VAL_DIM)
            oh = _rms(o[:n_tok, hh * GLA_VAL_DIM:(hh + 1) * GLA_VAL_DIM], gain)
            o_ref[0, rows, cols] = (oh * _silu(r_ref[0, rows, cols])).astype(BF16)
        st = st * s["decay"] + ds_t
        if sub == n_sub - 1:
            s_sc[p] = st
            st_ref[0, p] = st


def _gla(qg, kg, vg, rg, la, s0_t, g_out):
    b, t, _ = qg.shape
    tb = GLA_TB
    tok = min(t, tb * GLA_STEP_BLOCKS)
    nblk = t // tok
    m_stack, lvl, n_levels = _gla_consts(tb, GLA_CHUNK if t % GLA_CHUNK == 0 else t)
    blk = lambda w: pl.BlockSpec((1, tok, w), lambda bb, j: (bb, j, 0))
    full = lambda a: pl.BlockSpec(a.shape, lambda bb, j: (0,) * a.ndim)
    st_spec = pl.BlockSpec((1,) + s0_t.shape[1:], lambda bb, j: (bb, 0, 0, 0))
    return pl.pallas_call(
        functools.partial(_gla_body, n_levels=n_levels),
        grid=(b, nblk),
        in_specs=[blk(GLA_QK_WIDTH), blk(GLA_QK_WIDTH), blk(GLA_V_WIDTH), blk(GLA_V_WIDTH),
                  blk(GLA_QK_WIDTH), st_spec, full(g_out), full(m_stack), full(lvl)],
        out_specs=[blk(GLA_V_WIDTH), st_spec],
        out_shape=[jax.ShapeDtypeStruct((b, t, GLA_V_WIDTH), BF16),
                   jax.ShapeDtypeStruct(s0_t.shape, F32)],
        scratch_shapes=[pltpu.VMEM(s0_t.shape[1:], F32)],
        compiler_params=pltpu.CompilerParams(
            dimension_semantics=("parallel", "arbitrary"), vmem_limit_bytes=VMEM_LIMIT),
        name="gla",
    )(qg, kg, vg, rg, la, s0_t, g_out, m_stack, lvl)


def _state_to_t(s):
    b = s.shape[0]
    st = jnp.swapaxes(s, 2, 3).reshape(b, GLA_HEADS // 2, 2, GLA_VAL_DIM, GLA_KEY_DIM)
    z = jnp.zeros_like(st[:, :, 0])
    top = jnp.concatenate([st[:, :, 0], z], axis=-1)
    bot = jnp.concatenate([z, st[:, :, 1]], axis=-1)
    return jnp.concatenate([top, bot], axis=2)


def _state_from_t(st):
    b = st.shape[0]
    h0 = st[:, :, :GLA_VAL_DIM, :GLA_KEY_DIM]
    h1 = st[:, :, GLA_VAL_DIM:, GLA_KEY_DIM:]
    s = jnp.stack([h0, h1], axis=2).reshape(b, GLA_HEADS, GLA_VAL_DIM, GLA_KEY_DIM)
    return jnp.swapaxes(s, 2, 3)


def _post_body(x_ref, osb_ref, og_ref, ga1_ref, sh2_ref, sc2_ref, ga2_ref, gffn_ref, gfin_ref,
               wo_ref, wup_ref, wdn_ref, y_ref):
    mix = (jnp.dot(osb_ref[...], wo_ref[:SB_WIDTH, :], preferred_element_type=F32)
           + jnp.dot(og_ref[...], wo_ref[SB_WIDTH:, :], preferred_element_type=F32))
    x1 = x_ref[...] + ga1_ref[0] * mix
    h2 = (_rms(x1, gffn_ref[...]) * (1.0 + sc2_ref[0]) + sh2_ref[0]).astype(BF16)
    ff = jnp.zeros_like(x1)
    for c in range(D_FF // FF_CHUNK):
        u = jnp.dot(h2, wup_ref[:, c * FF_CHUNK:(c + 1) * FF_CHUNK], preferred_element_type=F32)
        a = jnp.square(jnp.maximum(u, 0.0)).astype(BF16)
        ff = ff + jnp.dot(a, wdn_ref[c * FF_CHUNK:(c + 1) * FF_CHUNK, :], preferred_element_type=F32)
    x2 = x1 + ga2_ref[0] * ff
    y_ref[...] = _rms(x2, gfin_ref[...])


def _post(x2d, o_sb, o_g, ga1, sh2, sc2, ga2, g_ffn, g_final, w_out, w_up, w_down, tm, rows_per_group):
    m, d = x2d.shape
    row = lambda w: pl.BlockSpec((tm, w), lambda i: (i, 0))
    full = lambda a: pl.BlockSpec(a.shape, lambda i: (0,) * a.ndim)
    resident = lambda a: pl.BlockSpec(a.shape, lambda i: (0,) * a.ndim, pipeline_mode=pl.Buffered(1))
    mod = lambda a: _mod_spec(a, tm, rows_per_group)
    return pl.pallas_call(
        _post_body,
        grid=(m // tm,),
        in_specs=[row(d), row(SB_WIDTH), row(GLA_V_WIDTH), mod(ga1), mod(sh2), mod(sc2), mod(ga2),
                  full(g_ffn), full(g_final), resident(w_out), resident(w_up), resident(w_down)],
        out_specs=row(d),
        out_shape=jax.ShapeDtypeStruct((m, d), F32),
        compiler_params=pltpu.CompilerParams(dimension_semantics=("parallel",),
                                             vmem_limit_bytes=VMEM_LIMIT),
        name="post_ffn",
    )(x2d, o_sb, o_g, ga1, sh2, sc2, ga2, g_ffn, g_final, w_out, w_up, w_down)


def _prep_weights(w_in, w_gla_gate, b_gla_gate, w_out, w_up, w_down):
    n_sb = 3 * SB_WIDTH
    n_gl = 2 * GLA_QK_WIDTH + 2 * GLA_V_WIDTH
    w_sb = w_in[:, :n_sb].astype(BF16)
    a_pad = jnp.pad(w_in[:, n_sb + n_gl:], ((0, 0), (0, LANES - GLA_GATE_RANK)))
    w_gl = jnp.concatenate([w_in[:, n_sb:n_sb + n_gl], a_pad], axis=1).astype(BF16)
    w_gate = jnp.pad(w_gla_gate, ((0, LANES - GLA_GATE_RANK), (0, 0))).astype(BF16)
    return (w_sb, w_gl, w_gate, b_gla_gate.reshape(1, -1),
            w_out.astype(BF16), w_up.astype(BF16), w_down.astype(BF16))


def _layer(x, mods, weights, norms, sb_fn, s0_t, tm):
    g_rows, t, d = x.shape
    sh1, sc1, ga1, sh2, sc2, ga2 = mods
    w_sb, w_gl, w_gate, b_gate, w_out, w_up, w_down = weights
    g_mix, g_gla_out, g_ffn, g_final = norms
    x2d = x.reshape(g_rows * t, d)
    rows_per_group = t if sh1.shape[1] == 1 else g_rows * t
    q, k, v, k_heads, v_heads, qg, kg, vg, rg, la = _inproj(
        x2d, sh1, sc1, g_mix, w_sb, w_gl, w_gate, b_gate, tm, rows_per_group)
    r3 = lambda a: a.reshape(g_rows, t, a.shape[-1])
    o_sb = sb_fn(r3(q), r3(k), r3(v))
    o_g, st = _gla(r3(qg), r3(kg), r3(vg), r3(rg), r3(la), s0_t, g_gla_out)
    y = _post(x2d, o_sb.reshape(g_rows * t, -1), o_g.reshape(g_rows * t, -1), ga1, sh2, sc2, ga2,
              g_ffn, g_final, w_out, w_up, w_down, tm, rows_per_group)
    return (y.reshape(g_rows, t, d), k_heads.reshape(g_rows, t, SB_HEADS, SB_HEAD_DIM),
            v_heads.reshape(g_rows, t, SB_HEADS, SB_HEAD_DIM), _state_from_t(st))


def kernel(x_prompt, x_sample, c_prompt, c_sample, cache_k, cache_v, state_gla, page_table, w_ada, b_ada, g_mix, w_in, b_sb, w_gla_gate, b_gla_gate, g_gla_out, w_out, g_ffn, w_up, w_down, g_final):
    depth = w_ada.shape[0]
    assert depth == 1, "final-norm fusion assumes a single layer"
    nb, t, d = x_prompt.shape
    nd, td, _ = x_sample.shape
    l = 0

    c_all = jnp.concatenate([c_prompt, c_sample], axis=0)
    pad_rows = (-c_all.shape[0]) % SUBLANES
    mod = _ada(jnp.pad(c_all, ((0, pad_rows), (0, 0))), w_ada[l], b_ada[l])
    mods_p = [m[:nb].reshape(nb, 1, d) for m in jnp.split(mod, 6, axis=-1)]
    mods_s = [jnp.repeat(m[nb:nb + nd], td, axis=0).reshape(1, nd * td, d)
              for m in jnp.split(mod, 6, axis=-1)]

    weights = _prep_weights(w_in[l], w_gla_gate[l], b_gla_gate[l], w_out[l], w_up[l], w_down[l])
    norms = (g_mix[l].reshape(1, d), g_gla_out[l].reshape(1, -1), g_ffn[l].reshape(1, d),
             g_final.reshape(1, d))

    sb_p = lambda q, k, v: _sb_prompt(q, k, v, b_sb[l])
    zeros_t = jnp.zeros((nb, GLA_HEADS // 2, 2 * GLA_VAL_DIM, 2 * GLA_KEY_DIM), F32)
    y_p, k_p, v_p, s_p = _layer(x_prompt, mods_p, weights, norms, sb_p, zeros_t, min(ROW_TILE, t))

    sb_s = lambda q, k, v: _sb_decode(q, k, v, cache_k[l], cache_v[l], page_table, b_sb[l])
    y_s, k_s, v_s, s_s = _layer(x_sample, mods_s, weights, norms, sb_s, _state_to_t(state_gla[l]),
                                nd * td)

    return (y_p, y_s, k_p[None], v_p[None], s_p[None], k_s[None], v_s[None], s_s[None])
```

```python
import functools

import numpy as np
import jax
import jax.numpy as jnp
from jax import lax
from jax.experimental import pallas as pl
from jax.experimental.pallas import tpu as pltpu

F32 = jnp.float32
BF16 = jnp.bfloat16

D_MODEL = 1024
SB_HEADS = 8
SB_HEAD_DIM = 64
SB_WIDTH = SB_HEADS * SB_HEAD_DIM
GLA_HEADS = 4
GLA_KEY_DIM = 64
GLA_VAL_DIM = 128
GLA_QK_WIDTH = GLA_HEADS * GLA_KEY_DIM
GLA_V_WIDTH = GLA_HEADS * GLA_VAL_DIM
GLA_GATE_RANK = 16
GLA_GATE_TEMP = 16.0
GLA_CHUNK = 16
D_FF = 4 * D_MODEL
NORM_EPS = 1e-6

LANES = 128
SUBLANES = 8
VMEM_LIMIT = 56 * 1024 * 1024

SB_BQ = 512
SB_BK = 256
GLA_TB = 128
GLA_STEP_BLOCKS = 4
DEC_PAGES_PER_STEP = 32
ROW_TILE = 1024
ADA_TN = 1024
FF_CHUNK = 1024
SB_MASKED = -1e30

_NT = (((1,), (1,)), ((), ()))
_TN = (((0,), (0,)), ((), ()))


LOG2E = float(np.log2(np.e))


EXP2_MAX = 126.0


def _softplus2(z2):
    return jnp.maximum(z2, jnp.log(1.0 + jnp.exp2(jnp.minimum(z2, EXP2_MAX))) * LOG2E)


def _silu(x):
    return x * jax.nn.sigmoid(x)


def _rms(x, gain):
    ms = jnp.mean(x * x, axis=-1, keepdims=True)
    return x * lax.rsqrt(ms + NORM_EPS) * gain


def _ada_body(c_ref, w_ref, b_ref, o_ref):
    s = _silu(c_ref[...])
    o_ref[...] = jnp.dot(s.astype(BF16), w_ref[...].astype(BF16),
                         preferred_element_type=F32) + b_ref[...]


def _ada(c, w_ada, b_ada):
    rows, d = c.shape
    n = w_ada.shape[1]
    return pl.pallas_call(
        _ada_body,
        grid=(n // ADA_TN,),
        in_specs=[pl.BlockSpec((rows, d), lambda j: (0, 0)),
                  pl.BlockSpec((d, ADA_TN), lambda j: (0, j)),
                  pl.BlockSpec((1, ADA_TN), lambda j: (0, j))],
        out_specs=pl.BlockSpec((rows, ADA_TN), lambda j: (0, j)),
        out_shape=jax.ShapeDtypeStruct((rows, n), F32),
        name="ada_mod",
    )(c, w_ada, b_ada.reshape(1, n))


def _rows_to_heads(x):
    rows = x.shape[0]
    assert SB_HEADS == SUBLANES
    groups = rows // SUBLANES
    t = [x[:, h * SB_HEAD_DIM:(h + 1) * SB_HEAD_DIM].reshape(groups, SUBLANES, SB_HEAD_DIM)
         for h in range(SB_HEADS)]
    sub = lax.broadcasted_iota(jnp.int32, t[0].shape, 1)
    for dist in (4, 2, 1):
        keep = (sub & dist) == 0
        y = list(t)
        for v in range(SB_HEADS):
            if v & dist == 0:
                a, b = t[v], t[v + dist]
                y[v] = jnp.where(keep, a, pltpu.roll(b, dist, axis=1))
                y[v + dist] = jnp.where(keep, pltpu.roll(a, SUBLANES - dist, axis=1), b)
        t = y
    return jnp.stack(t, axis=1).reshape(rows, SB_HEADS, SB_HEAD_DIM)


def _inproj_body(x_ref, sh_ref, sc_ref, g_ref, wsb_ref, wgl_ref, wgate_ref, bgate_ref,
                 q_ref, k_ref, v_ref, kh_ref, vh_ref, qg_ref, kg_ref, vg_ref, rg_ref, la_ref):
    h = _rms(x_ref[...], g_ref[...]) * (1.0 + sc_ref[0]) + sh_ref[0]
    hb = h.astype(BF16)
    p1 = jnp.dot(hb, wsb_ref[...], preferred_element_type=F32)
    q_ref[...] = (p1[:, :SB_WIDTH] * (LOG2E * SB_HEAD_DIM ** -0.5)).astype(BF16)
    k = p1[:, SB_WIDTH:2 * SB_WIDTH]
    v = p1[:, 2 * SB_WIDTH:3 * SB_WIDTH]
    k_ref[...] = k.astype(BF16)
    v_ref[...] = v.astype(BF16)
    kh_ref[...] = _rows_to_heads(k)
    vh_ref[...] = _rows_to_heads(v)
    p2 = jnp.dot(hb, wgl_ref[...], preferred_element_type=F32)
    o = 0
    qg_ref[...] = p2[:, o:o + GLA_QK_WIDTH] * (GLA_KEY_DIM ** -0.5)
    o += GLA_QK_WIDTH
    kg_ref[...] = p2[:, o:o + GLA_QK_WIDTH]
    o += GLA_QK_WIDTH
    vg_ref[...] = p2[:, o:o + GLA_V_WIDTH].astype(BF16)
    o += GLA_V_WIDTH
    rg_ref[...] = p2[:, o:o + GLA_V_WIDTH]
    o += GLA_V_WIDTH
    a_low = p2[:, o:o + LANES]
    xg = jnp.dot(a_low.astype(BF16), wgate_ref[...], preferred_element_type=F32) + bgate_ref[...]
    la_ref[...] = (jnp.minimum(xg, 0.0) - jnp.log(1.0 + jnp.exp(-jnp.abs(xg)))) * (LOG2E / GLA_GATE_TEMP)


def _mod_spec(mod, tm, rows_per_group):
    if mod.shape[1] == 1:
        tiles_per_group = rows_per_group // tm
        return pl.BlockSpec((1, 1, mod.shape[2]), lambda i: (i // tiles_per_group, 0, 0))
    return pl.BlockSpec((1, tm, mod.shape[2]), lambda i: (0, i, 0))


def _inproj(x2d, sh, sc, g_mix, w_sb, w_gl, w_gate, b_gate, tm, rows_per_group):
    m, d = x2d.shape
    row = lambda w: pl.BlockSpec((tm, w), lambda i: (i, 0))
    full = lambda a: pl.BlockSpec(a.shape, lambda i: (0,) * a.ndim)
    heads = ((SB_HEADS, SB_HEAD_DIM), F32)
    outs = [(SB_WIDTH, BF16), (SB_WIDTH, BF16), (SB_WIDTH, BF16), heads, heads, (GLA_QK_WIDTH, F32),
            (GLA_QK_WIDTH, F32), (GLA_V_WIDTH, BF16), (GLA_V_WIDTH, F32), (GLA_QK_WIDTH, F32)]
    tail = lambda w: w if isinstance(w, tuple) else (w,)
    out_spec = lambda w: pl.BlockSpec((tm,) + tail(w), lambda i: (i,) + (0,) * len(tail(w)))
    return pl.pallas_call(
        _inproj_body,
        grid=(m // tm,),
        in_specs=[row(d), _mod_spec(sh, tm, rows_per_group), _mod_spec(sc, tm, rows_per_group),
                  full(g_mix), full(w_sb), full(w_gl), full(w_gate), full(b_gate)],
        out_specs=[out_spec(w) for w, _ in outs],
        out_shape=[jax.ShapeDtypeStruct((m,) + tail(w), dt) for w, dt in outs],
        compiler_params=pltpu.CompilerParams(dimension_semantics=("parallel",),
                                             vmem_limit_bytes=VMEM_LIMIT),
        name="in_proj",
    )(x2d, sh, sc, g_mix, w_sb, w_gl, w_gate, b_gate)


def _sb_prompt_body(bias_ref, q_ref, k_ref, v_ref, u_ref, o_ref,
                    mask_sc, z_sc, e_sc, carry_sc, acc_sc):
    bq = q_ref.shape[0]
    bk = u_ref.shape[0]
    r = bq // bk
    assert r == 2, "the two hand-over iterations are also the two tiles crossing the diagonal"
    p = pl.program_id(1)
    i = pl.program_id(2)
    live = i < pl.num_programs(2) - 1
    n = r * (i + 1)
    last_tile = k_ref.shape[0] // bk - 1

    @pl.when(i == 0)
    def _():
        row = lax.broadcasted_iota(jnp.int32, (bq, bk), 0)
        col = lax.broadcasted_iota(jnp.int32, (bq, bk), 1)
        dead = jnp.full((bq, bk), SB_MASKED, F32)
        for t in range(r):
            mask_sc[t] = jnp.where((r - 1 - t) * bk + col < row, 0.0, SB_MASKED)
        mask_sc[r] = dead
        for hh in range(2):
            z_sc[1, hh] = dead
            e_sc[0, hh] = dead.astype(BF16)
            carry_sc[hh] = jnp.zeros((bq, LANES), F32)
        acc_sc[...] = jnp.zeros((bq, LANES), F32)

    q = q_ref[...]
    lane = lax.broadcasted_iota(jnp.int32, (bq, LANES), 1)
    lane_k = lax.broadcasted_iota(jnp.int32, (bk, LANES), 1)
    first_head = lane < SB_HEAD_DIM
    first_head_k = lane_k < SB_HEAD_DIM
    qh, bias_lanes, bias_rows = [], [], []
    for hh in range(2):
        lo_lane = SB_HEAD_DIM * (1 - hh)
        ones = jnp.where((lane == lo_lane) | (lane == lo_lane + 1), 1.0, 0.0).astype(BF16)
        qh.append(jnp.where(first_head, q, ones) if hh == 0 else jnp.where(first_head, ones, q))
        b = jnp.full((bk, LANES), bias_ref[2 * p + hh] * LOG2E, F32)
        b_hi = b.astype(BF16).astype(F32)
        bias_rows.append(jnp.where(lane_k == lo_lane, b_hi, b - b_hi).astype(BF16))
        in_pair = jnp.where((lane_k == lo_lane) | (lane_k == lo_lane + 1), 1.0, 0.0)
        bias_lanes.append(in_pair.astype(BF16) > 0)

    def tile_start(j):
        return pl.multiple_of(j * bk, bk)

    def iteration(t, slot, on_diagonal=False):
        kb = k_ref[pl.ds(tile_start(jnp.minimum(r * i + r - 1 - t, last_tile)), bk), :]
        for hh in range(2):
            z = lax.dot_general(qh[hh], jnp.where(bias_lanes[hh], bias_rows[hh], kb), _NT,
                                preferred_element_type=F32)
            if on_diagonal:
                z = mask_sc[jnp.where(live, t, r)] + z
            z_sc[slot, hh] = z
        for hh in range(2):
            z = z_sc[1 - slot, hh]
            c = jnp.concatenate([carry_sc[hh]] * (bk // LANES), axis=1)
            sp = _softplus2(z)
            later = jnp.dot(sp.astype(BF16), u_ref[...], preferred_element_type=F32) + c
            e_sc[1 - slot, hh] = ((z - sp) - later).astype(BF16)
            carry_sc[hh] = jnp.broadcast_to(later[:, 0:1] + sp[:, 0:1], (bq, LANES))
        jv = jnp.where(t < 2, 1 - t, r * i + r + 1 - t)
        vb = v_ref[pl.ds(tile_start(jv), bk), :]
        zero = jnp.zeros_like(vb)
        v2 = jnp.concatenate([jnp.where(first_head_k, vb, zero), jnp.where(first_head_k, zero, vb)],
                             axis=0)
        w2 = jnp.concatenate([jnp.exp2(e_sc[slot, 0]), jnp.exp2(e_sc[slot, 1])], axis=1)
        acc_sc[...] += jnp.dot(w2, v2, preferred_element_type=F32)

    iteration(0, 0, on_diagonal=True)
    carry_sc[...] = jnp.zeros(carry_sc.shape, F32)
    iteration(1, 1, on_diagonal=True)

    @pl.when(i > 0)
    def _():
        o_ref[...] = acc_sc[...].astype(BF16)
    acc_sc[...] = jnp.zeros((bq, LANES), F32)

    def body(tt, _):
        for s in range(4):
            iteration(2 + 4 * tt + s, s % 2)
        return 0

    rest = jnp.where(live, n - 2, 0)
    lax.fori_loop(0, rest // 4, body, 0)

    @pl.when(rest % 4 != 0)
    def _():
        iteration(rest, 0)
        iteration(rest + 1, 1)


def _later_ones(n):
    r = np.arange(n)
    return jnp.asarray((r[:, None] > r[None, :]).astype(np.float32), dtype=BF16)


def _sb_prompt(q, k, v, b_sb):
    b, t, w = q.shape
    bq = min(SB_BQ, t)
    bk = min(SB_BK, t)
    pairs = w // LANES
    nq = t // bq
    return pl.pallas_call(
        _sb_prompt_body,
        grid=(b, pairs, nq + 1),
        in_specs=[pl.BlockSpec(memory_space=pltpu.SMEM),
                  pl.BlockSpec((None, bq, LANES), lambda bb, p, i: (bb, jnp.minimum(i, nq - 1), p)),
                  pl.BlockSpec((None, t, LANES), lambda bb, p, i: (bb, 0, p)),
                  pl.BlockSpec((None, t, LANES), lambda bb, p, i: (bb, 0, p)),
                  pl.BlockSpec((bk, bk), lambda bb, p, i: (0, 0))],
        out_specs=pl.BlockSpec((None, bq, LANES), lambda bb, p, i: (bb, jnp.maximum(i - 1, 0), p)),
        out_shape=jax.ShapeDtypeStruct((b, t, w), BF16),
        scratch_shapes=[pltpu.VMEM((bq // bk + 1, bq, bk), F32),
                        pltpu.VMEM((2, 2, bq, bk), F32),
                        pltpu.VMEM((2, 2, bq, bk), BF16),
                        pltpu.VMEM((2, bq, LANES), F32),
                        pltpu.VMEM((bq, LANES), F32)],
        compiler_params=pltpu.CompilerParams(
            dimension_semantics=("parallel", "parallel", "arbitrary"),
            vmem_limit_bytes=VMEM_LIMIT),
        name="sb_prompt",
    )(b_sb, q, k, v, _later_ones(bk))


def _sb_decode_body(pt_ref, bias_ref, q_ref, kn_ref, vn_ref, *rest):
    del pt_ref
    npg = DEC_PAGES_PER_STEP
    k_pages = rest[:npg]
    v_pages = rest[npg:2 * npg]
    u_ref, o_ref, carry_sc, acc_sc = rest[2 * npg:]
    g = pl.program_id(1)
    n_tok, width = q_ref.shape[1:]
    rows = SB_HEADS * n_tok
    page = k_pages[0].shape[2]

    row = lax.broadcasted_iota(jnp.int32, (rows, width), 0)
    lane = lax.broadcasted_iota(jnp.int32, (rows, width), 1)
    own = (row // n_tok) == (lane // SB_HEAD_DIM)
    qbd = jnp.where(own, jnp.concatenate([q_ref[0].astype(F32)] * SB_HEADS, axis=0), 0.0).astype(BF16)
    row_p = lax.broadcasted_iota(jnp.int32, (rows, page), 0)
    col_p = lax.broadcasted_iota(jnp.int32, (rows, page), 1)
    bias = jnp.zeros((rows, page), F32)
    for h in range(SB_HEADS):
        bias = jnp.where(row_p // n_tok == h, bias_ref[h] * LOG2E, bias)
    u = u_ref[...]

    @pl.when(g == 0)
    def _():
        pad = jnp.zeros((page - n_tok, width), F32)
        kb = jnp.concatenate([kn_ref[0].astype(F32), pad], axis=0).astype(BF16)
        vb = jnp.concatenate([vn_ref[0].astype(F32), pad], axis=0).astype(BF16)
        causal = col_p < (row_p % n_tok)
        z = lax.dot_general(qbd, kb, _NT, preferred_element_type=F32) + bias
        sp = jnp.where(causal, _softplus2(z), 0.0)
        later = jnp.dot(sp.astype(BF16), u, preferred_element_type=F32)
        w = jnp.where(causal, jnp.exp2((z - sp) - later), 0.0)
        carry_sc[...] = jnp.broadcast_to(later[:, 0:1] + sp[:, 0:1], carry_sc.shape)
        acc_sc[...] = jnp.dot(w.astype(BF16), vb, preferred_element_type=F32)

    order = list(reversed(range(npg)))
    as_matrix = lambda ref: ref[...].reshape(width, page).astype(BF16)
    zs = [jnp.dot(qbd, as_matrix(k_pages[n]), preferred_element_type=F32) + bias for n in order]
    z_all = jnp.concatenate(zs, axis=0)
    sp = _softplus2(z_all)
    log_beta = z_all - sp
    later_all = jnp.dot(sp.astype(BF16), u, preferred_element_type=F32)
    c = carry_sc[...]
    acc = acc_sc[...]
    for i, n in enumerate(order):
        pg = slice(i * rows, (i + 1) * rows)
        w = jnp.exp2(log_beta[pg] - later_all[pg] - c)
        c = c + jnp.broadcast_to(later_all[pg][:, 0:1] + sp[pg][:, 0:1], c.shape)
        acc = acc + lax.dot_general(w.astype(BF16), as_matrix(v_pages[n]), _NT,
                                    preferred_element_type=F32)
    carry_sc[...] = c
    acc_sc[...] = acc

    @pl.when(g == pl.num_programs(1) - 1)
    def _():
        own_acc = jnp.where(own, acc, 0.0)
        out = own_acc[0:n_tok]
        for h in range(1, SB_HEADS):
            out = out + own_acc[h * n_tok:(h + 1) * n_tok]
        o_ref[0] = out.astype(BF16)


def _sb_decode(q, k_new, v_new, cache_k, cache_v, page_table, b_sb):
    nb, n_tok, w = q.shape
    n_pages = page_table.shape[1]
    page = cache_k.shape[1]
    npg = DEC_PAGES_PER_STEP
    groups = n_pages // npg
    rows = SB_HEADS * n_tok
    cache_k = jnp.transpose(cache_k, (0, 2, 3, 1))
    cache_v = jnp.transpose(cache_v, (0, 2, 3, 1))

    def page_map(n):
        return lambda b, g, pt: (pt[b, (groups - 1 - g) * npg + n], 0, 0, 0)

    tok_spec = pl.BlockSpec((1, n_tok, w), lambda b, g, pt: (b, 0, 0))
    page_specs = [pl.BlockSpec((None, SB_HEADS, SB_HEAD_DIM, page), page_map(n)) for n in range(npg)]
    grid_spec = pltpu.PrefetchScalarGridSpec(
        num_scalar_prefetch=1,
        grid=(nb, groups),
        in_specs=[pl.BlockSpec(memory_space=pltpu.SMEM), tok_spec, tok_spec, tok_spec]
        + page_specs + page_specs
        + [pl.BlockSpec((page, page), lambda b, g, pt: (0, 0))],
        out_specs=tok_spec,
        scratch_shapes=[pltpu.VMEM((rows, LANES), F32), pltpu.VMEM((rows, w), F32)],
    )
    return pl.pallas_call(
        _sb_decode_body,
        grid_spec=grid_spec,
        out_shape=jax.ShapeDtypeStruct((nb, n_tok, w), BF16),
        compiler_params=pltpu.CompilerParams(
            dimension_semantics=("parallel", "arbitrary"), vmem_limit_bytes=VMEM_LIMIT),
        name="sb_decode",
    )(page_table, b_sb, q, k_new, v_new, *([cache_k] * npg), *([cache_v] * npg), _later_ones(page))


def _gla_consts(tb, chunk):
    i = np.arange(tb)[:, None]
    t = np.arange(tb)[None, :]
    mats = []
    lvl = np.zeros((tb, tb), np.float32)
    code = 1
    s = tb
    while s > chunk:
        mid = (i // s) * s + s // 2
        upper = i >= mid
        mats.append(np.where(upper & (t >= mid) & (t <= i), 1.0, 0.0)
                    + np.where(~upper & (t > i) & (t < mid), -1.0, 0.0))
        mid_t = (t // s) * s + s // 2
        lvl = np.where((i // s == t // s) & upper & (t < mid_t), code, lvl)
        code += 1
        s //= 2
    mats.append(np.where((t // chunk == i // chunk) & (t <= i), 1.0, 0.0))
    lvl = np.where((i // chunk == t // chunk) & (t <= i), code, lvl)
    mats.append(np.where(t <= i, 1.0, 0.0))
    stack = np.concatenate(mats, axis=0).astype(np.float32)
    return jnp.asarray(stack, dtype=BF16), jnp.asarray(lvl, dtype=F32), code


def _gla_body(q_ref, k_ref, v_ref, r_ref, la_ref, s0_ref, g_ref, m_ref, lvl_ref,
              o_ref, st_ref, s_sc, *, n_levels):
    tb = lvl_ref.shape[0]
    n_tok = min(q_ref.shape[1], tb)
    n_sub = max(q_ref.shape[1] // tb, 1)
    blk = pl.program_id(1)

    @pl.when(blk == 0)
    def _():
        s_sc[...] = s0_ref[0]

    def load(ref, sub, lo, width):
        x = ref[0, sub * n_tok:(sub + 1) * n_tok, lo:lo + width]
        if n_tok < tb:
            x = jnp.concatenate([x.astype(F32), jnp.zeros((tb - n_tok, width), F32)], axis=0).astype(x.dtype)
        return x

    at_level = [lvl_ref[...] == float(n + 1) for n in range(n_levels)]
    lane = lax.broadcasted_iota(jnp.int32, (tb, LANES), 1)
    first_head = lane < GLA_KEY_DIM
    row_s = lax.broadcasted_iota(jnp.int32, (2 * GLA_VAL_DIM, LANES), 0)
    lane_s = lax.broadcasted_iota(jnp.int32, (2 * GLA_VAL_DIM, LANES), 1)
    own_state = (row_s < GLA_VAL_DIM) == (lane_s < GLA_KEY_DIM)
    gain = g_ref[...]

    blocks = [(p, sub) for p in range(GLA_HEADS // 2) for sub in range(n_sub)]

    def decays(p, sub):
        la = load(la_ref, sub, p * LANES, LANES)
        hi = la.astype(BF16)
        lo = (la - hi.astype(F32)).astype(BF16)
        res = jnp.dot(m_ref[...], jnp.concatenate([hi, lo], axis=1), preferred_element_type=F32)
        dall = res[:, :LANES] + res[:, LANES:]
        return [dall[n * tb:(n + 1) * tb] for n in range(n_levels + 1)]

    def scaled(p, sub, d):
        bc = d[n_levels]
        e_tot = bc[tb - 1:tb, :]
        q = load(q_ref, sub, p * LANES, LANES)
        k = load(k_ref, sub, p * LANES, LANES)
        qs = []
        for n in range(n_levels - 1):
            x = jnp.exp2(-jnp.abs(d[n]))
            qs.append(((q * x).astype(BF16), (k * x).astype(BF16)))
        qs.append(((q * jnp.exp2(d[n_levels - 1])).astype(BF16),
                   (k * jnp.exp2(-d[n_levels - 1])).astype(BF16)))
        return dict(qk=qs, q_hat=(q * jnp.exp2(bc)).astype(BF16),
                    k_hat=(k * jnp.exp2(e_tot - bc)).astype(BF16), decay=jnp.exp2(e_tot))

    def scores(s):
        out = []
        for hh in range(2):
            head = first_head if hh == 0 else jnp.logical_not(first_head)
            sc = jnp.zeros((tb, tb), F32)
            for n, (qn, kn) in enumerate(s["qk"]):
                s_n = lax.dot_general(jnp.where(head, qn, jnp.zeros_like(qn)), kn, _NT,
                                      preferred_element_type=F32)
                sc = jnp.where(at_level[n], s_n, sc)
            out.append(sc.astype(BF16))
        return out

    def values(p, sub, s, sc):
        o_intra = jnp.concatenate(
            [jnp.dot(sc[hh], load(v_ref, sub, (2 * p + hh) * GLA_VAL_DIM, GLA_VAL_DIM),
                     preferred_element_type=F32) for hh in range(2)], axis=1)
        v_pair = load(v_ref, sub, 2 * p * GLA_VAL_DIM, 2 * GLA_VAL_DIM)
        ds_t = lax.dot_general(v_pair, s["k_hat"], _TN, preferred_element_type=F32)
        return o_intra, jnp.where(own_state, ds_t, 0.0)

    ds = [decays(p, sub) for p, sub in blocks]
    ss = [scaled(p, sub, d) for (p, sub), d in zip(blocks, ds)]
    scs = [scores(s) for s in ss]
    vals = [values(p, sub, s, sc) for (p, sub), s, sc in zip(blocks, ss, scs)]

    st = None
    for (p, sub), s, (o_intra, ds_t) in zip(blocks, ss, vals):
        if sub == 0:
            st = s_sc[p]
        o = o_intra + lax.dot_general(s["q_hat"], st.astype(BF16), _NT, preferred_element_type=F32)
        rows = slice(sub * n_tok, (sub + 1) * n_tok)
        for hh in range(2):
            cols = slice((2 * p + hh) * GLA_VAL_DIM, (2 * p + hh + 1) * GLA_VAL_DIM)
            oh = _rms(o[:n_tok, hh * GLA_VAL_DIM:(hh + 1) * GLA_VAL_DIM], gain)
            o_ref[0, rows, cols] = (oh * _silu(r_ref[0, rows, cols])).astype(BF16)
        st = st * s["decay"] + ds_t
        if sub == n_sub - 1:
            s_sc[p] = st
            st_ref[0, p] = st


def _gla(qg, kg, vg, rg, la, s0_t, g_out):
    b, t, _ = qg.shape
    tb = GLA_TB
    tok = min(t, tb * GLA_STEP_BLOCKS)
    nblk = t // tok
    m_stack, lvl, n_levels = _gla_consts(tb, GLA_CHUNK if t % GLA_CHUNK == 0 else t)
    blk = lambda w: pl.BlockSpec((1, tok, w), lambda bb, j: (bb, j, 0))
    full = lambda a: pl.BlockSpec(a.shape, lambda bb, j: (0,) * a.ndim)
    st_spec = pl.BlockSpec((1,) + s0_t.shape[1:], lambda bb, j: (bb, 0, 0, 0))
    return pl.pallas_call(
        functools.partial(_gla_body, n_levels=n_levels),
        grid=(b, nblk),
        in_specs=[blk(GLA_QK_WIDTH), blk(GLA_QK_WIDTH), blk(GLA_V_WIDTH), blk(GLA_V_WIDTH),
                  blk(GLA_QK_WIDTH), st_spec, full(g_out), full(m_stack), full(lvl)],
        out_specs=[blk(GLA_V_WIDTH), st_spec],
        out_shape=[jax.ShapeDtypeStruct((b, t, GLA_V_WIDTH), BF16),
                   jax.ShapeDtypeStruct(s0_t.shape, F32)],
        scratch_shapes=[pltpu.VMEM(s0_t.shape[1:], F32)],
        compiler_params=pltpu.CompilerParams(
            dimension_semantics=("parallel", "arbitrary"), vmem_limit_bytes=VMEM_LIMIT),
        name="gla",
    )(qg, kg, vg, rg, la, s0_t, g_out, m_stack, lvl)


def _state_to_t(s):
    b = s.shape[0]
    st = jnp.swapaxes(s, 2, 3).reshape(b, GLA_HEADS // 2, 2, GLA_VAL_DIM, GLA_KEY_DIM)
    z = jnp.zeros_like(st[:, :, 0])
    top = jnp.concatenate([st[:, :, 0], z], axis=-1)
    bot = jnp.concatenate([z, st[:, :, 1]], axis=-1)
    return jnp.concatenate([top, bot], axis=2)


def _state_from_t(st):
    b = st.shape[0]
    h0 = st[:, :, :GLA_VAL_DIM, :GLA_KEY_DIM]
    h1 = st[:, :, GLA_VAL_DIM:, GLA_KEY_DIM:]
    s = jnp.stack([h0, h1], axis=2).reshape(b, GLA_HEADS, GLA_VAL_DIM, GLA_KEY_DIM)
    return jnp.swapaxes(s, 2, 3)


def _post_body(x_ref, osb_ref, og_ref, ga1_ref, sh2_ref, sc2_ref, ga2_ref, gffn_ref, gfin_ref,
               wo_ref, wup_ref, wdn_ref, y_ref):
    mix = (jnp.dot(osb_ref[...], wo_ref[:SB_WIDTH, :], preferred_element_type=F32)
           + jnp.dot(og_ref[...], wo_ref[SB_WIDTH:, :], preferred_element_type=F32))
    x1 = x_ref[...] + ga1_ref[0] * mix
    h2 = (_rms(x1, gffn_ref[...]) * (1.0 + sc2_ref[0]) + sh2_ref[0]).astype(BF16)
    ff = jnp.zeros_like(x1)
    for c in range(D_FF // FF_CHUNK):
        u = jnp.dot(h2, wup_ref[:, c * FF_CHUNK:(c + 1) * FF_CHUNK], preferred_element_type=F32)
        a = jnp.square(jnp.maximum(u, 0.0)).astype(BF16)
        ff = ff + jnp.dot(a, wdn_ref[c * FF_CHUNK:(c + 1) * FF_CHUNK, :], preferred_element_type=F32)
    x2 = x1 + ga2_ref[0] * ff
    y_ref[...] = _rms(x2, gfin_ref[...])


def _post(x2d, o_sb, o_g, ga1, sh2, sc2, ga2, g_ffn, g_final, w_out, w_up, w_down, tm, rows_per_group):
    m, d = x2d.shape
    row = lambda w: pl.BlockSpec((tm, w), lambda i: (i, 0))
    full = lambda a: pl.BlockSpec(a.shape, lambda i: (0,) * a.ndim)
    resident = lambda a: pl.BlockSpec(a.shape, lambda i: (0,) * a.ndim, pipeline_mode=pl.Buffered(1))
    mod = lambda a: _mod_spec(a, tm, rows_per_group)
    return pl.pallas_call(
        _post_body,
        grid=(m // tm,),
        in_specs=[row(d), row(SB_WIDTH), row(GLA_V_WIDTH), mod(ga1), mod(sh2), mod(sc2), mod(ga2),
                  full(g_ffn), full(g_final), resident(w_out), resident(w_up), resident(w_down)],
        out_specs=row(d),
        out_shape=jax.ShapeDtypeStruct((m, d), F32),
        compiler_params=pltpu.CompilerParams(dimension_semantics=("parallel",),
                                             vmem_limit_bytes=VMEM_LIMIT),
        name="post_ffn",
    )(x2d, o_sb, o_g, ga1, sh2, sc2, ga2, g_ffn, g_final, w_out, w_up, w_down)


def _prep_weights(w_in, w_gla_gate, b_gla_gate, w_out, w_up, w_down):
    n_sb = 3 * SB_WIDTH
    n_gl = 2 * GLA_QK_WIDTH + 2 * GLA_V_WIDTH
    w_sb = w_in[:, :n_sb].astype(BF16)
    a_pad = jnp.pad(w_in[:, n_sb + n_gl:], ((0, 0), (0, LANES - GLA_GATE_RANK)))
    w_gl = jnp.concatenate([w_in[:, n_sb:n_sb + n_gl], a_pad], axis=1).astype(BF16)
    w_gate = jnp.pad(w_gla_gate, ((0, LANES - GLA_GATE_RANK), (0, 0))).astype(BF16)
    return (w_sb, w_gl, w_gate, b_gla_gate.reshape(1, -1),
            w_out.astype(BF16), w_up.astype(BF16), w_down.astype(BF16))


def _layer(x, mods, weights, norms, sb_fn, s0_t, tm):
    g_rows, t, d = x.shape
    sh1, sc1, ga1, sh2, sc2, ga2 = mods
    w_sb, w_gl, w_gate, b_gate, w_out, w_up, w_down = weights
    g_mix, g_gla_out, g_ffn, g_final = norms
    x2d = x.reshape(g_rows * t, d)
    rows_per_group = t if sh1.shape[1] == 1 else g_rows * t
    q, k, v, k_heads, v_heads, qg, kg, vg, rg, la = _inproj(
        x2d, sh1, sc1, g_mix, w_sb, w_gl, w_gate, b_gate, tm, rows_per_group)
    r3 = lambda a: a.reshape(g_rows, t, a.shape[-1])
    o_sb = sb_fn(r3(q), r3(k), r3(v))
    o_g, st = _gla(r3(qg), r3(kg), r3(vg), r3(rg), r3(la), s0_t, g_gla_out)
    y = _post(x2d, o_sb.reshape(g_rows * t, -1), o_g.reshape(g_rows * t, -1), ga1, sh2, sc2, ga2,
              g_ffn, g_final, w_out, w_up, w_down, tm, rows_per_group)
    return (y.reshape(g_rows, t, d), k_heads.reshape(g_rows, t, SB_HEADS, SB_HEAD_DIM),
            v_heads.reshape(g_rows, t, SB_HEADS, SB_HEAD_DIM), _state_from_t(st))


def kernel(x_prompt, x_sample, c_prompt, c_sample, cache_k, cache_v, state_gla, page_table, w_ada, b_ada, g_mix, w_in, b_sb, w_gla_gate, b_gla_gate, g_gla_out, w_out, g_ffn, w_up, w_down, g_final):
    depth = w_ada.shape[0]
    assert depth == 1, "final-norm fusion assumes a single layer"
    nb, t, d = x_prompt.shape
    nd, td, _ = x_sample.shape
    l = 0

    c_all = jnp.concatenate([c_prompt, c_sample], axis=0)
    pad_rows = (-c_all.shape[0]) % SUBLANES
    mod = _ada(jnp.pad(c_all, ((0, pad_rows), (0, 0))), w_ada[l], b_ada[l])
    mods_p = [m[:nb].reshape(nb, 1, d) for m in jnp.split(mod, 6, axis=-1)]
    mods_s = [jnp.repeat(m[nb:nb + nd], td, axis=0).reshape(1, nd * td, d)
              for m in jnp.split(mod, 6, axis=-1)]

    weights = _prep_weights(w_in[l], w_gla_gate[l], b_gla_gate[l], w_out[l], w_up[l], w_down[l])
    norms = (g_mix[l].reshape(1, d), g_gla_out[l].reshape(1, -1), g_ffn[l].reshape(1, d),
             g_final.reshape(1, d))

    sb_p = lambda q, k, v: _sb_prompt(q, k, v, b_sb[l])
    zeros_t = jnp.zeros((nb, GLA_HEADS // 2, 2 * GLA_VAL_DIM, 2 * GLA_KEY_DIM), F32)
    y_p, k_p, v_p, s_p = _layer(x_prompt, mods_p, weights, norms, sb_p, zeros_t, min(ROW_TILE, t))

    sb_s = lambda q, k, v: _sb_decode(q, k, v, cache_k[l], cache_v[l], page_table, b_sb[l])
    y_s, k_s, v_s, s_s = _layer(x_sample, mods_s, weights, norms, sb_s, _state_to_t(state_gla[l]),
                                nd * td)

    return (y_p, y_s, k_p[None], v_p[None], s_p[None], k_s[None], v_s[None], s_s[None])
```

```python
import functools

import numpy as np
import jax
import jax.numpy as jnp
from jax import lax
from jax.experimental import pallas as pl
from jax.experimental.pallas import tpu as pltpu

F32 = jnp.float32
BF16 = jnp.bfloat16

D_MODEL = 1024
SB_HEADS = 8
SB_HEAD_DIM = 64
SB_WIDTH = SB_HEADS * SB_HEAD_DIM
GLA_HEADS = 4
GLA_KEY_DIM = 64
GLA_VAL_DIM = 128
GLA_QK_WIDTH = GLA_HEADS * GLA_KEY_DIM
GLA_V_WIDTH = GLA_HEADS * GLA_VAL_DIM
GLA_GATE_RANK = 16
GLA_GATE_TEMP = 16.0
GLA_CHUNK = 16
D_FF = 4 * D_MODEL
NORM_EPS = 1e-6

LANES = 128
SUBLANES = 8
VMEM_LIMIT = 56 * 1024 * 1024

SB_BQ = 512
SB_BK = 256
GLA_TB = 128
GLA_STEP_BLOCKS = 4
DEC_PAGES_PER_STEP = 32
ROW_TILE = 1024
ADA_TN = 1024
FF_CHUNK = 1024
SB_MASKED = -1e30

_NT = (((1,), (1,)), ((), ()))
_TN = (((0,), (0,)), ((), ()))


LOG2E = float(np.log2(np.e))


EXP2_MAX = 126.0


def _softplus2(z2):
    return jnp.maximum(z2, jnp.log(1.0 + jnp.exp2(jnp.minimum(z2, EXP2_MAX))) * LOG2E)


def _silu(x):
    return x * jax.nn.sigmoid(x)


def _rms(x, gain):
    ms = jnp.mean(x * x, axis=-1, keepdims=True)
    return x * lax.rsqrt(ms + NORM_EPS) * gain


def _ada_body(c_ref, w_ref, b_ref, o_ref):
    s = _silu(c_ref[...])
    o_ref[...] = jnp.dot(s.astype(BF16), w_ref[...].astype(BF16),
                         preferred_element_type=F32) + b_ref[...]


def _ada(c, w_ada, b_ada):
    rows, d = c.shape
    n = w_ada.shape[1]
    return pl.pallas_call(
        _ada_body,
        grid=(n // ADA_TN,),
        in_specs=[pl.BlockSpec((rows, d), lambda j: (0, 0)),
                  pl.BlockSpec((d, ADA_TN), lambda j: (0, j)),
                  pl.BlockSpec((1, ADA_TN), lambda j: (0, j))],
        out_specs=pl.BlockSpec((rows, ADA_TN), lambda j: (0, j)),
        out_shape=jax.ShapeDtypeStruct((rows, n), F32),
        name="ada_mod",
    )(c, w_ada, b_ada.reshape(1, n))


def _rows_to_heads(x):
    rows = x.shape[0]
    assert SB_HEADS == SUBLANES
    groups = rows // SUBLANES
    t = [x[:, h * SB_HEAD_DIM:(h + 1) * SB_HEAD_DIM].reshape(groups, SUBLANES, SB_HEAD_DIM)
         for h in range(SB_HEADS)]
    sub = lax.broadcasted_iota(jnp.int32, t[0].shape, 1)
    for dist in (4, 2, 1):
        keep = (sub & dist) == 0
        y = list(t)
        for v in range(SB_HEADS):
            if v & dist == 0:
                a, b = t[v], t[v + dist]
                y[v] = jnp.where(keep, a, pltpu.roll(b, dist, axis=1))
                y[v + dist] = jnp.where(keep, pltpu.roll(a, SUBLANES - dist, axis=1), b)
        t = y
    return jnp.stack(t, axis=1).reshape(rows, SB_HEADS, SB_HEAD_DIM)


def _inproj_body(x_ref, sh_ref, sc_ref, g_ref, wsb_ref, wgl_ref, wgate_ref, bgate_ref,
                 q_ref, k_ref, v_ref, kh_ref, vh_ref, qg_ref, kg_ref, vg_ref, rg_ref, la_ref):
    h = _rms(x_ref[...], g_ref[...]) * (1.0 + sc_ref[0]) + sh_ref[0]
    hb = h.astype(BF16)
    p1 = jnp.dot(hb, wsb_ref[...], preferred_element_type=F32)
    q_ref[...] = (p1[:, :SB_WIDTH] * (LOG2E * SB_HEAD_DIM ** -0.5)).astype(BF16)
    k = p1[:, SB_WIDTH:2 * SB_WIDTH]
    v = p1[:, 2 * SB_WIDTH:3 * SB_WIDTH]
    k_ref[...] = k.astype(BF16)
    v_ref[...] = v.astype(BF16)
    kh_ref[...] = _rows_to_heads(k)
    vh_ref[...] = _rows_to_heads(v)
    p2 = jnp.dot(hb, wgl_ref[...], preferred_element_type=F32)
    o = 0
    qg_ref[...] = p2[:, o:o + GLA_QK_WIDTH] * (GLA_KEY_DIM ** -0.5)
    o += GLA_QK_WIDTH
    kg_ref[...] = p2[:, o:o + GLA_QK_WIDTH]
    o += GLA_QK_WIDTH
    vg_ref[...] = p2[:, o:o + GLA_V_WIDTH].astype(BF16)
    o += GLA_V_WIDTH
    rg_ref[...] = p2[:, o:o + GLA_V_WIDTH]
    o += GLA_V_WIDTH
    a_low = p2[:, o:o + LANES]
    xg = jnp.dot(a_low.astype(BF16), wgate_ref[...], preferred_element_type=F32) + bgate_ref[...]
    la_ref[...] = (jnp.minimum(xg, 0.0) - jnp.log(1.0 + jnp.exp(-jnp.abs(xg)))) * (LOG2E / GLA_GATE_TEMP)


def _mod_spec(mod, tm, rows_per_group):
    if mod.shape[1] == 1:
        tiles_per_group = rows_per_group // tm
        return pl.BlockSpec((1, 1, mod.shape[2]), lambda i: (i // tiles_per_group, 0, 0))
    return pl.BlockSpec((1, tm, mod.shape[2]), lambda i: (0, i, 0))


def _inproj(x2d, sh, sc, g_mix, w_sb, w_gl, w_gate, b_gate, tm, rows_per_group):
    m, d = x2d.shape
    row = lambda w: pl.BlockSpec((tm, w), lambda i: (i, 0))
    full = lambda a: pl.BlockSpec(a.shape, lambda i: (0,) * a.ndim)
    heads = ((SB_HEADS, SB_HEAD_DIM), F32)
    outs = [(SB_WIDTH, BF16), (SB_WIDTH, BF16), (SB_WIDTH, BF16), heads, heads, (GLA_QK_WIDTH, F32),
            (GLA_QK_WIDTH, F32), (GLA_V_WIDTH, BF16), (GLA_V_WIDTH, F32), (GLA_QK_WIDTH, F32)]
    tail = lambda w: w if isinstance(w, tuple) else (w,)
    out_spec = lambda w: pl.BlockSpec((tm,) + tail(w), lambda i: (i,) + (0,) * len(tail(w)))
    return pl.pallas_call(
        _inproj_body,
        grid=(m // tm,),
        in_specs=[row(d), _mod_spec(sh, tm, rows_per_group), _mod_spec(sc, tm, rows_per_group),
                  full(g_mix), full(w_sb), full(w_gl), full(w_gate), full(b_gate)],
        out_specs=[out_spec(w) for w, _ in outs],
        out_shape=[jax.ShapeDtypeStruct((m,) + tail(w), dt) for w, dt in outs],
        compiler_params=pltpu.CompilerParams(dimension_semantics=("parallel",),
                                             vmem_limit_bytes=VMEM_LIMIT),
        name="in_proj",
    )(x2d, sh, sc, g_mix, w_sb, w_gl, w_gate, b_gate)


def _sb_prompt_body(bias_ref, q_ref, k_ref, v_ref, u_ref, o_ref,
                    mask_sc, z_sc, e_sc, carry_sc, acc_sc):
    bq = q_ref.shape[0]
    bk = u_ref.shape[0]
    r = bq // bk
    assert r == 2, "the two hand-over iterations are also the two tiles crossing the diagonal"
    p = pl.program_id(1)
    i = pl.program_id(2)
    live = i < pl.num_programs(2) - 1
    n = r * (i + 1)
    last_tile = k_ref.shape[0] // bk - 1

    @pl.when(i == 0)
    def _():
        row = lax.broadcasted_iota(jnp.int32, (bq, bk), 0)
        col = lax.broadcasted_iota(jnp.int32, (bq, bk), 1)
        dead = jnp.full((bq, bk), SB_MASKED, F32)
        for t in range(r):
            mask_sc[t] = jnp.where((r - 1 - t) * bk + col < row, 0.0, SB_MASKED)
        mask_sc[r] = dead
        for hh in range(2):
            z_sc[1, hh] = dead
            e_sc[0, hh] = dead.astype(BF16)
            carry_sc[hh] = jnp.zeros((bq, LANES), F32)
        acc_sc[...] = jnp.zeros((bq, LANES), F32)

    q = q_ref[...]
    lane = lax.broadcasted_iota(jnp.int32, (bq, LANES), 1)
    lane_k = lax.broadcasted_iota(jnp.int32, (bk, LANES), 1)
    first_head = lane < SB_HEAD_DIM
    first_head_k = lane_k < SB_HEAD_DIM
    qh, bias_lanes, bias_rows = [], [], []
    for hh in range(2):
        lo_lane = SB_HEAD_DIM * (1 - hh)
        ones = jnp.where((lane == lo_lane) | (lane == lo_lane + 1), 1.0, 0.0).astype(BF16)
        qh.append(jnp.where(first_head, q, ones) if hh == 0 else jnp.where(first_head, ones, q))
        b = jnp.full((bk, LANES), bias_ref[2 * p + hh] * LOG2E, F32)
        b_hi = b.astype(BF16).astype(F32)
        bias_rows.append(jnp.where(lane_k == lo_lane, b_hi, b - b_hi).astype(BF16))
        in_pair = jnp.where((lane_k == lo_lane) | (lane_k == lo_lane + 1), 1.0, 0.0)
        bias_lanes.append(in_pair.astype(BF16) > 0)

    def tile_start(j):
        return pl.multiple_of(j * bk, bk)

    def iteration(t, slot, on_diagonal=False):
        kb = k_ref[pl.ds(tile_start(jnp.minimum(r * i + r - 1 - t, last_tile)), bk), :]
        for hh in range(2):
            z = lax.dot_general(qh[hh], jnp.where(bias_lanes[hh], bias_rows[hh], kb), _NT,
                                preferred_element_type=F32)
            if on_diagonal:
                z = mask_sc[jnp.where(live, t, r)] + z
            z_sc[slot, hh] = z
        for hh in range(2):
            z = z_sc[1 - slot, hh]
            c = jnp.concatenate([carry_sc[hh]] * (bk // LANES), axis=1)
            sp = _softplus2(z)
            later = jnp.dot(sp.astype(BF16), u_ref[...], preferred_element_type=F32) + c
            e_sc[1 - slot, hh] = ((z - sp) - later).astype(BF16)
            carry_sc[hh] = jnp.broadcast_to(later[:, 0:1] + sp[:, 0:1], (bq, LANES))
        jv = jnp.where(t < 2, 1 - t, r * i + r + 1 - t)
        vb = v_ref[pl.ds(tile_start(jv), bk), :]
        zero = jnp.zeros_like(vb)
        v2 = jnp.concatenate([jnp.where(first_head_k, vb, zero), jnp.where(first_head_k, zero, vb)],
                             axis=0)
        w2 = jnp.concatenate([jnp.exp2(e_sc[slot, 0]), jnp.exp2(e_sc[slot, 1])], axis=1)
        acc_sc[...] += jnp.dot(w2, v2, preferred_element_type=F32)

    iteration(0, 0, on_diagonal=True)
    carry_sc[...] = jnp.zeros(carry_sc.shape, F32)
    iteration(1, 1, on_diagonal=True)

    @pl.when(i > 0)
    def _():
        o_ref[...] = acc_sc[...].astype(BF16)
    acc_sc[...] = jnp.zeros((bq, LANES), F32)

    def body(tt, _):
        for s in range(4):
            iteration(2 + 4 * tt + s, s % 2)
        return 0

    rest = jnp.where(live, n - 2, 0)
    lax.fori_loop(0, rest // 4, body, 0)

    @pl.when(rest % 4 != 0)
    def _():
        iteration(rest, 0)
        iteration(rest + 1, 1)


def _later_ones(n):
    r = np.arange(n)
    return jnp.asarray((r[:, None] > r[None, :]).astype(np.float32), dtype=BF16)


def _sb_prompt(q, k, v, b_sb):
    b, t, w = q.shape
    bq = min(SB_BQ, t)
    bk = min(SB_BK, t)
    pairs = w // LANES
    nq = t // bq
    return pl.pallas_call(
        _sb_prompt_body,
        grid=(b, pairs, nq + 1),
        in_specs=[pl.BlockSpec(memory_space=pltpu.SMEM),
                  pl.BlockSpec((None, bq, LANES), lambda bb, p, i: (bb, jnp.minimum(i, nq - 1), p)),
                  pl.BlockSpec((None, t, LANES), lambda bb, p, i: (bb, 0, p)),
                  pl.BlockSpec((None, t, LANES), lambda bb, p, i: (bb, 0, p)),
                  pl.BlockSpec((bk, bk), lambda bb, p, i: (0, 0))],
        out_specs=pl.BlockSpec((None, bq, LANES), lambda bb, p, i: (bb, jnp.maximum(i - 1, 0), p)),
        out_shape=jax.ShapeDtypeStruct((b, t, w), BF16),
        scratch_shapes=[pltpu.VMEM((bq // bk + 1, bq, bk), F32),
                        pltpu.VMEM((2, 2, bq, bk), F32),
                        pltpu.VMEM((2, 2, bq, bk), BF16),
                        pltpu.VMEM((2, bq, LANES), F32),
                        pltpu.VMEM((bq, LANES), F32)],
        compiler_params=pltpu.CompilerParams(
            dimension_semantics=("parallel", "parallel", "arbitrary"),
            vmem_limit_bytes=VMEM_LIMIT),
        name="sb_prompt",
    )(b_sb, q, k, v, _later_ones(bk))


def _sb_decode_body(pt_ref, bias_ref, q_ref, kn_ref, vn_ref, *rest):
    del pt_ref
    npg = DEC_PAGES_PER_STEP
    k_pages = rest[:npg]
    v_pages = rest[npg:2 * npg]
    u_ref, o_ref, carry_sc, acc_sc = rest[2 * npg:]
    g = pl.program_id(1)
    n_tok, width = q_ref.shape[1:]
    rows = SB_HEADS * n_tok
    page = k_pages[0].shape[2]

    row = lax.broadcasted_iota(jnp.int32, (rows, width), 0)
    lane = lax.broadcasted_iota(jnp.int32, (rows, width), 1)
    own = (row // n_tok) == (lane // SB_HEAD_DIM)
    qbd = jnp.where(own, jnp.concatenate([q_ref[0].astype(F32)] * SB_HEADS, axis=0), 0.0).astype(BF16)
    row_p = lax.broadcasted_iota(jnp.int32, (rows, page), 0)
    col_p = lax.broadcasted_iota(jnp.int32, (rows, page), 1)
    bias = jnp.zeros((rows, page), F32)
    for h in range(SB_HEADS):
        bias = jnp.where(row_p // n_tok == h, bias_ref[h] * LOG2E, bias)
    u = u_ref[...]

    @pl.when(g == 0)
    def _():
        pad = jnp.zeros((page - n_tok, width), F32)
        kb = jnp.concatenate([kn_ref[0].astype(F32), pad], axis=0).astype(BF16)
        vb = jnp.concatenate([vn_ref[0].astype(F32), pad], axis=0).astype(BF16)
        causal = col_p < (row_p % n_tok)
        z = lax.dot_general(qbd, kb, _NT, preferred_element_type=F32) + bias
        sp = jnp.where(causal, _softplus2(z), 0.0)
        later = jnp.dot(sp.astype(BF16), u, preferred_element_type=F32)
        w = jnp.where(causal, jnp.exp2((z - sp) - later), 0.0)
        carry_sc[...] = jnp.broadcast_to(later[:, 0:1] + sp[:, 0:1], carry_sc.shape)
        acc_sc[...] = jnp.dot(w.astype(BF16), vb, preferred_element_type=F32)

    order = list(reversed(range(npg)))
    as_matrix = lambda ref: ref[...].reshape(width, page).astype(BF16)
    zs = [jnp.dot(qbd, as_matrix(k_pages[n]), preferred_element_type=F32) + bias for n in order]
    z_all = jnp.concatenate(zs, axis=0)
    sp = _softplus2(z_all)
    log_beta = z_all - sp
    later_all = jnp.dot(sp.astype(BF16), u, preferred_element_type=F32)
    c = carry_sc[...]
    acc = acc_sc[...]
    for i, n in enumerate(order):
        pg = slice(i * rows, (i + 1) * rows)
        w = jnp.exp2(log_beta[pg] - later_all[pg] - c)
        c = c + jnp.broadcast_to(later_all[pg][:, 0:1] + sp[pg][:, 0:1], c.shape)
        acc = acc + lax.dot_general(w.astype(BF16), as_matrix(v_pages[n]), _NT,
                                    preferred_element_type=F32)
    carry_sc[...] = c
    acc_sc[...] = acc

    @pl.when(g == pl.num_programs(1) - 1)
    def _():
        own_acc = jnp.where(own, acc, 0.0)
        out = own_acc[0:n_tok]
        for h in range(1, SB_HEADS):
            out = out + own_acc[h * n_tok:(h + 1) * n_tok]
        o_ref[0] = out.astype(BF16)


def _sb_decode(q, k_new, v_new, cache_k, cache_v, page_table, b_sb):
    nb, n_tok, w = q.shape
    n_pages = page_table.shape[1]
    page = cache_k.shape[1]
    npg = DEC_PAGES_PER_STEP
    groups = n_pages // npg
    rows = SB_HEADS * n_tok
    cache_k = jnp.transpose(cache_k, (0, 2, 3, 1))
    cache_v = jnp.transpose(cache_v, (0, 2, 3, 1))

    def page_map(n):
        return lambda b, g, pt: (pt[b, (groups - 1 - g) * npg + n], 0, 0, 0)

    tok_spec = pl.BlockSpec((1, n_tok, w), lambda b, g, pt: (b, 0, 0))
    page_specs = [pl.BlockSpec((None, SB_HEADS, SB_HEAD_DIM, page), page_map(n)) for n in range(npg)]
    grid_spec = pltpu.PrefetchScalarGridSpec(
        num_scalar_prefetch=1,
        grid=(nb, groups),
        in_specs=[pl.BlockSpec(memory_space=pltpu.SMEM), tok_spec, tok_spec, tok_spec]
        + page_specs + page_specs
        + [pl.BlockSpec((page, page), lambda b, g, pt: (0, 0))],
        out_specs=tok_spec,
        scratch_shapes=[pltpu.VMEM((rows, LANES), F32), pltpu.VMEM((rows, w), F32)],
    )
    return pl.pallas_call(
        _sb_decode_body,
        grid_spec=grid_spec,
        out_shape=jax.ShapeDtypeStruct((nb, n_tok, w), BF16),
        compiler_params=pltpu.CompilerParams(
            dimension_semantics=("parallel", "arbitrary"), vmem_limit_bytes=VMEM_LIMIT),
        name="sb_decode",
    )(page_table, b_sb, q, k_new, v_new, *([cache_k] * npg), *([cache_v] * npg), _later_ones(page))


def _gla_consts(tb, chunk):
    i = np.arange(tb)[:, None]
    t = np.arange(tb)[None, :]
    mats = []
    lvl = np.zeros((tb, tb), np.float32)
    code = 1
    s = tb
    while s > chunk:
        mid = (i // s) * s + s // 2
        upper = i >= mid
        mats.append(np.where(upper & (t >= mid) & (t <= i), 1.0, 0.0)
                    + np.where(~upper & (t > i) & (t < mid), -1.0, 0.0))
        mid_t = (t // s) * s + s // 2
        lvl = np.where((i // s == t // s) & upper & (t < mid_t), code, lvl)
        code += 1
        s //= 2
    mats.append(np.where((t // chunk == i // chunk) & (t <= i), 1.0, 0.0))
    lvl = np.where((i // chunk == t // chunk) & (t <= i), code, lvl)
    mats.append(np.where(t <= i, 1.0, 0.0))
    stack = np.concatenate(mats, axis=0).astype(np.float32)
    return jnp.asarray(stack, dtype=BF16), jnp.asarray(lvl, dtype=F32), code


def _gla_body(q_ref, k_ref, v_ref, r_ref, la_ref, s0_ref, g_ref, m_ref, lvl_ref,
              o_ref, st_ref, s_sc, *, n_levels):
    tb = lvl_ref.shape[0]
    n_tok = min(q_ref.shape[1], tb)
    n_sub = max(q_ref.shape[1] // tb, 1)
    n_seq = q_ref.shape[0]
    blk = pl.program_id(1)

    @pl.when(blk == 0)
    def _():
        s_sc[...] = s0_ref[...]

    def load(ref, unit, lo, width):
        seq, sub = divmod(unit, n_sub)
        x = ref[seq, sub * n_tok:(sub + 1) * n_tok, lo:lo + width]
        if n_tok < tb:
            x = jnp.concatenate([x.astype(F32), jnp.zeros((tb - n_tok, width), F32)], axis=0).astype(x.dtype)
        return x

    at_level = [lvl_ref[...] == float(n + 1) for n in range(n_levels)]
    lane = lax.broadcasted_iota(jnp.int32, (tb, LANES), 1)
    first_head = lane < GLA_KEY_DIM
    row_s = lax.broadcasted_iota(jnp.int32, (2 * GLA_VAL_DIM, LANES), 0)
    lane_s = lax.broadcasted_iota(jnp.int32, (2 * GLA_VAL_DIM, LANES), 1)
    own_state = (row_s < GLA_VAL_DIM) == (lane_s < GLA_KEY_DIM)
    gain = g_ref[...]

    blocks = [(p, unit) for p in range(GLA_HEADS // 2) for unit in range(n_seq * n_sub)]

    def decays(p, sub):
        la = load(la_ref, sub, p * LANES, LANES)
        hi = la.astype(BF16)
        lo = (la - hi.astype(F32)).astype(BF16)
        res = jnp.dot(m_ref[...], jnp.concatenate([hi, lo], axis=1), preferred_element_type=F32)
        dall = res[:, :LANES] + res[:, LANES:]
        return [dall[n * tb:(n + 1) * tb] for n in range(n_levels + 1)]

    def scaled(p, sub, d):
        bc = d[n_levels]
        e_tot = bc[tb - 1:tb, :]
        q = load(q_ref, sub, p * LANES, LANES)
        k = load(k_ref, sub, p * LANES, LANES)
        qs = []
        for n in range(n_levels - 1):
            x = jnp.exp2(-jnp.abs(d[n]))
            qs.append(((q * x).astype(BF16), (k * x).astype(BF16)))
        qs.append(((q * jnp.exp2(d[n_levels - 1])).astype(BF16),
                   (k * jnp.exp2(-d[n_levels - 1])).astype(BF16)))
        return dict(qk=qs, q_hat=(q * jnp.exp2(bc)).astype(BF16),
                    k_hat=(k * jnp.exp2(e_tot - bc)).astype(BF16), decay=jnp.exp2(e_tot))

    def scores(s):
        out = []
        for hh in range(2):
            head = first_head if hh == 0 else jnp.logical_not(first_head)
            sc = jnp.zeros((tb, tb), F32)
            for n, (qn, kn) in enumerate(s["qk"]):
                s_n = lax.dot_general(jnp.where(head, qn, jnp.zeros_like(qn)), kn, _NT,
                                      preferred_element_type=F32)
                sc = jnp.where(at_level[n], s_n, sc)
            out.append(sc.astype(BF16))
        return out

    def values(p, sub, s, sc):
        o_intra = jnp.concatenate(
            [jnp.dot(sc[hh], load(v_ref, sub, (2 * p + hh) * GLA_VAL_DIM, GLA_VAL_DIM),
                     preferred_element_type=F32) for hh in range(2)], axis=1)
        v_pair = load(v_ref, sub, 2 * p * GLA_VAL_DIM, 2 * GLA_VAL_DIM)
        ds_t = lax.dot_general(v_pair, s["k_hat"], _TN, preferred_element_type=F32)
        return o_intra, jnp.where(own_state, ds_t, 0.0)

    ds = [decays(p, sub) for p, sub in blocks]
    ss = [scaled(p, sub, d) for (p, sub), d in zip(blocks, ds)]
    scs = [scores(s) for s in ss]
    vals = [values(p, sub, s, sc) for (p, sub), s, sc in zip(blocks, ss, scs)]

    st = None
    for (p, unit), s, (o_intra, ds_t) in zip(blocks, ss, vals):
        seq, sub = divmod(unit, n_sub)
        if sub == 0:
            st = s_sc[seq, p]
        o = o_intra + lax.dot_general(s["q_hat"], st.astype(BF16), _NT, preferred_element_type=F32)
        rows = slice(sub * n_tok, (sub + 1) * n_tok)
        for hh in range(2):
            cols = slice((2 * p + hh) * GLA_VAL_DIM, (2 * p + hh + 1) * GLA_VAL_DIM)
            oh = _rms(o[:n_tok, hh * GLA_VAL_DIM:(hh + 1) * GLA_VAL_DIM], gain)
            o_ref[seq, rows, cols] = (oh * _silu(r_ref[seq, rows, cols])).astype(BF16)
        st = st * s["decay"] + ds_t
        if sub == n_sub - 1:
            s_sc[seq, p] = st
            st_ref[seq, p] = st


def _gla(qg, kg, vg, rg, la, s0_t, g_out):
    b, t, _ = qg.shape
    tb = GLA_TB
    tok = min(t, tb * GLA_STEP_BLOCKS)
    nblk = t // tok
    seqs = GLA_STEP_BLOCKS if (nblk == 1 and tok <= tb and b % GLA_STEP_BLOCKS == 0) else 1
    m_stack, lvl, n_levels = _gla_consts(tb, GLA_CHUNK if t % GLA_CHUNK == 0 else t)
    blk = lambda w: pl.BlockSpec((seqs, tok, w), lambda bb, j: (bb, j, 0))
    full = lambda a: pl.BlockSpec(a.shape, lambda bb, j: (0,) * a.ndim)
    st_spec = pl.BlockSpec((seqs,) + s0_t.shape[1:], lambda bb, j: (bb, 0, 0, 0))
    return pl.pallas_call(
        functools.partial(_gla_body, n_levels=n_levels),
        grid=(b // seqs, nblk),
        in_specs=[blk(GLA_QK_WIDTH), blk(GLA_QK_WIDTH), blk(GLA_V_WIDTH), blk(GLA_V_WIDTH),
                  blk(GLA_QK_WIDTH), st_spec, full(g_out), full(m_stack), full(lvl)],
        out_specs=[blk(GLA_V_WIDTH), st_spec],
        out_shape=[jax.ShapeDtypeStruct((b, t, GLA_V_WIDTH), BF16),
                   jax.ShapeDtypeStruct(s0_t.shape, F32)],
        scratch_shapes=[pltpu.VMEM((seqs,) + s0_t.shape[1:], F32)],
        compiler_params=pltpu.CompilerParams(
            dimension_semantics=("parallel", "arbitrary"), vmem_limit_bytes=VMEM_LIMIT),
        name="gla",
    )(qg, kg, vg, rg, la, s0_t, g_out, m_stack, lvl)


def _state_to_t(s):
    b = s.shape[0]
    st = jnp.swapaxes(s, 2, 3).reshape(b, GLA_HEADS // 2, 2, GLA_VAL_DIM, GLA_KEY_DIM)
    z = jnp.zeros_like(st[:, :, 0])
    top = jnp.concatenate([st[:, :, 0], z], axis=-1)
    bot = jnp.concatenate([z, st[:, :, 1]], axis=-1)
    return jnp.concatenate([top, bot], axis=2)


def _state_from_t(st):
    b = st.shape[0]
    h0 = st[:, :, :GLA_VAL_DIM, :GLA_KEY_DIM]
    h1 = st[:, :, GLA_VAL_DIM:, GLA_KEY_DIM:]
    s = jnp.stack([h0, h1], axis=2).reshape(b, GLA_HEADS, GLA_VAL_DIM, GLA_KEY_DIM)
    return jnp.swapaxes(s, 2, 3)


def _post_body(x_ref, osb_ref, og_ref, ga1_ref, sh2_ref, sc2_ref, ga2_ref, gffn_ref, gfin_ref,
               wo_ref, wup_ref, wdn_ref, y_ref):
    mix = (jnp.dot(osb_ref[...], wo_ref[:SB_WIDTH, :], preferred_element_type=F32)
           + jnp.dot(og_ref[...], wo_ref[SB_WIDTH:, :], preferred_element_type=F32))
    x1 = x_ref[...] + ga1_ref[0] * mix
    h2 = (_rms(x1, gffn_ref[...]) * (1.0 + sc2_ref[0]) + sh2_ref[0]).astype(BF16)
    ff = jnp.zeros_like(x1)
    for c in range(D_FF // FF_CHUNK):
        u = jnp.dot(h2, wup_ref[:, c * FF_CHUNK:(c + 1) * FF_CHUNK], preferred_element_type=F32)
        a = jnp.square(jnp.maximum(u, 0.0)).astype(BF16)
        ff = ff + jnp.dot(a, wdn_ref[c * FF_CHUNK:(c + 1) * FF_CHUNK, :], preferred_element_type=F32)
    x2 = x1 + ga2_ref[0] * ff
    y_ref[...] = _rms(x2, gfin_ref[...])


def _post(x2d, o_sb, o_g, ga1, sh2, sc2, ga2, g_ffn, g_final, w_out, w_up, w_down, tm, rows_per_group):
    m, d = x2d.shape
    row = lambda w: pl.BlockSpec((tm, w), lambda i: (i, 0))
    full = lambda a: pl.BlockSpec(a.shape, lambda i: (0,) * a.ndim)
    resident = lambda a: pl.BlockSpec(a.shape, lambda i: (0,) * a.ndim, pipeline_mode=pl.Buffered(1))
    mod = lambda a: _mod_spec(a, tm, rows_per_group)
    return pl.pallas_call(
        _post_body,
        grid=(m // tm,),
        in_specs=[row(d), row(SB_WIDTH), row(GLA_V_WIDTH), mod(ga1), mod(sh2), mod(sc2), mod(ga2),
                  full(g_ffn), full(g_final), resident(w_out), resident(w_up), resident(w_down)],
        out_specs=row(d),
        out_shape=jax.ShapeDtypeStruct((m, d), F32),
        compiler_params=pltpu.CompilerParams(dimension_semantics=("parallel",),
                                             vmem_limit_bytes=VMEM_LIMIT),
        name="post_ffn",
    )(x2d, o_sb, o_g, ga1, sh2, sc2, ga2, g_ffn, g_final, w_out, w_up, w_down)


def _prep_weights(w_in, w_gla_gate, b_gla_gate, w_out, w_up, w_down):
    n_sb = 3 * SB_WIDTH
    n_gl = 2 * GLA_QK_WIDTH + 2 * GLA_V_WIDTH
    w_sb = w_in[:, :n_sb].astype(BF16)
    a_pad = jnp.pad(w_in[:, n_sb + n_gl:], ((0, 0), (0, LANES - GLA_GATE_RANK)))
    w_gl = jnp.concatenate([w_in[:, n_sb:n_sb + n_gl], a_pad], axis=1).astype(BF16)
    w_gate = jnp.pad(w_gla_gate, ((0, LANES - GLA_GATE_RANK), (0, 0))).astype(BF16)
    return (w_sb, w_gl, w_gate, b_gla_gate.reshape(1, -1),
            w_out.astype(BF16), w_up.astype(BF16), w_down.astype(BF16))


def _layer(x, mods, weights, norms, sb_fn, s0_t, tm):
    g_rows, t, d = x.shape
    sh1, sc1, ga1, sh2, sc2, ga2 = mods
    w_sb, w_gl, w_gate, b_gate, w_out, w_up, w_down = weights
    g_mix, g_gla_out, g_ffn, g_final = norms
    x2d = x.reshape(g_rows * t, d)
    rows_per_group = t if sh1.shape[1] == 1 else g_rows * t
    q, k, v, k_heads, v_heads, qg, kg, vg, rg, la = _inproj(
        x2d, sh1, sc1, g_mix, w_sb, w_gl, w_gate, b_gate, tm, rows_per_group)
    r3 = lambda a: a.reshape(g_rows, t, a.shape[-1])
    o_sb = sb_fn(r3(q), r3(k), r3(v))
    o_g, st = _gla(r3(qg), r3(kg), r3(vg), r3(rg), r3(la), s0_t, g_gla_out)
    y = _post(x2d, o_sb.reshape(g_rows * t, -1), o_g.reshape(g_rows * t, -1), ga1, sh2, sc2, ga2,
              g_ffn, g_final, w_out, w_up, w_down, tm, rows_per_group)
    return (y.reshape(g_rows, t, d), k_heads.reshape(g_rows, t, SB_HEADS, SB_HEAD_DIM),
            v_heads.reshape(g_rows, t, SB_HEADS, SB_HEAD_DIM), _state_from_t(st))


def kernel(x_prompt, x_sample, c_prompt, c_sample, cache_k, cache_v, state_gla, page_table, w_ada, b_ada, g_mix, w_in, b_sb, w_gla_gate, b_gla_gate, g_gla_out, w_out, g_ffn, w_up, w_down, g_final):
    depth = w_ada.shape[0]
    assert depth == 1, "final-norm fusion assumes a single layer"
    nb, t, d = x_prompt.shape
    nd, td, _ = x_sample.shape
    l = 0

    c_all = jnp.concatenate([c_prompt, c_sample], axis=0)
    pad_rows = (-c_all.shape[0]) % SUBLANES
    mod = _ada(jnp.pad(c_all, ((0, pad_rows), (0, 0))), w_ada[l], b_ada[l])
    mods_p = [m[:nb].reshape(nb, 1, d) for m in jnp.split(mod, 6, axis=-1)]
    mods_s = [jnp.repeat(m[nb:nb + nd], td, axis=0).reshape(1, nd * td, d)
              for m in jnp.split(mod, 6, axis=-1)]

    weights = _prep_weights(w_in[l], w_gla_gate[l], b_gla_gate[l], w_out[l], w_up[l], w_down[l])
    norms = (g_mix[l].reshape(1, d), g_gla_out[l].reshape(1, -1), g_ffn[l].reshape(1, d),
             g_final.reshape(1, d))

    sb_p = lambda q, k, v: _sb_prompt(q, k, v, b_sb[l])
    zeros_t = jnp.zeros((nb, GLA_HEADS // 2, 2 * GLA_VAL_DIM, 2 * GLA_KEY_DIM), F32)
    y_p, k_p, v_p, s_p = _layer(x_prompt, mods_p, weights, norms, sb_p, zeros_t, min(ROW_TILE, t))

    sb_s = lambda q, k, v: _sb_decode(q, k, v, cache_k[l], cache_v[l], page_table, b_sb[l])
    y_s, k_s, v_s, s_s = _layer(x_sample, mods_s, weights, norms, sb_s, _state_to_t(state_gla[l]),
                                nd * td)

    return (y_p, y_s, k_p[None], v_p[None], s_p[None], k_s[None], v_s[None], s_s[None])
```
